```python
import jax, jax.numpy as jnp
from jax import lax
import numpy as np

D_MODEL = 1024
BATCH = 2
SEQ = 16384
DEPTH = 1
DEC_BATCH = 8
DEC_SEQ = 8192
PAST_LEN = 128

GRID_W = 64
HG_HEADS = 4
HG_DK = 128
HG_DV = 128
HG_CHUNK = 64
HG_W = HG_HEADS * HG_DK
HG_VW = HG_HEADS * HG_DV
ATT_HEADS = 8
ATT_KV_HEADS = 2
ATT_HD = 128
ATT_GROUP = ATT_HEADS // ATT_KV_HEADS
ATT_W = ATT_HEADS * ATT_HD
ATT_KV_W = ATT_KV_HEADS * ATT_HD
Q_BLOCK = 128
ROPE_AXIS_DIM = ATT_HD // 2
ROPE_THETA = 10000.0
MEM_LEN = 256
X_HEADS = 4
X_HD = 128
X_W = X_HEADS * X_HD
N_EXPERTS = 16
EC_FACTOR = 2
D_EXPERT = 2048
EPS = 1e-6
IN_SIZES = (HG_W, HG_W, HG_W, HG_VW, HG_VW, ATT_W, ATT_KV_W, ATT_KV_W, D_MODEL, D_MODEL)
IN_WIDTH = HG_W * 3 + HG_VW * 2 + ATT_W + ATT_KV_W * 2 + D_MODEL * 2

kernel_name = 'hybrid_hgrn2_axial_gqa_ec_moe_encoder'

F32 = jnp.float32


def _rmsnorm(x, g):
    xf = x.astype(F32)
    y = xf * lax.rsqrt(jnp.mean(xf * xf, axis=-1, keepdims=True) + EPS)
    return (y * g.astype(F32)).astype(x.dtype)


def _split_cols(z, sizes):
    out, off = [], 0
    for s in sizes:
        out.append(z[..., off:off + s])
        off += s
    return out


def _gla_chunk_scan(q, k, v, log_f):
    n, t, h, dk = q.shape
    dv = v.shape[-1]
    nc = t // HG_CHUNK

    def to_chunks(a):
        return a.astype(F32).reshape(n, nc, HG_CHUNK, h, a.shape[-1]).transpose(1, 0, 3, 2, 4)

    qc, kc, vc, gc = to_chunks(q), to_chunks(k), to_chunks(v), to_chunks(log_f)
    causal = jnp.tril(jnp.ones((HG_CHUNK, HG_CHUNK), dtype=bool))[:, :, None]

    def step(state, inp):
        qi, ki, vi, gi = inp
        b = jnp.cumsum(gi, axis=2)
        o_inter = jnp.einsum('nhtk,nhkv->nhtv', qi * jnp.exp(b), state)
        diff = b[:, :, :, None, :] - b[:, :, None, :, :]
        decay = jnp.exp(jnp.where(causal, diff, -jnp.inf))
        scores = jnp.einsum('nhtk,nhsk,nhtsk->nhts', qi, ki, decay)
        o_intra = jnp.einsum('nhts,nhsv->nhtv', scores, vi)
        b_last = b[:, :, -1, :]
        state = jnp.exp(b_last)[..., None] * state + jnp.einsum(
            'nhsk,nhsv->nhkv', ki * jnp.exp(b_last[:, :, None, :] - b), vi)
        return state, o_intra + o_inter

    s0 = jnp.zeros((n, h, dk, dv), F32)
    _, o = lax.scan(step, s0, (qc, kc, vc, gc))
    return o.transpose(1, 0, 3, 2, 4).reshape(n, t, h, dv)


def _hgrn2_bidir(q, zf_fwd, zf_bwd, i, g, lb, gain):
    bsz, t, _ = q.shape

    def heads(a):
        return a.reshape(bsz, t, HG_HEADS, -1)

    def forget(z, lbd):
        f = lbd + (1.0 - lbd) * jax.nn.sigmoid(z.astype(F32))
        return heads(jnp.log(f)), heads(1.0 - f)

    logf_f, k_f = forget(zf_fwd, lb[0])
    logf_b, k_b = forget(zf_bwd, lb[1])
    qh, ih = heads(q.astype(F32)), heads(i.astype(F32))

    def rev(a):
        return a[:, ::-1]

    o2 = _gla_chunk_scan(
        jnp.concatenate([qh, rev(qh)], axis=0),
        jnp.concatenate([k_f, rev(k_b)], axis=0),
        jnp.concatenate([ih, rev(ih)], axis=0),
        jnp.concatenate([logf_f, rev(logf_b)], axis=0))
    o = o2[:bsz] + rev(o2[bsz:])
    o = o * lax.rsqrt(jnp.mean(o * o, axis=-1, keepdims=True) + EPS)
    o = o.reshape(bsz, t, HG_VW) * gain.astype(F32)
    return (o * jax.nn.silu(g.astype(F32))).astype(q.dtype)


def _axial_rope_tables(t):
    n_rows = t // GRID_W
    rows = jnp.repeat(jnp.arange(n_rows, dtype=jnp.int32), GRID_W)
    cols = jnp.arange(t, dtype=jnp.int32) % GRID_W
    pos = jnp.stack([rows, cols], axis=-1).astype(F32)
    inv = jnp.power(ROPE_THETA, -jnp.arange(0, ROPE_AXIS_DIM, 2, dtype=F32) / ROPE_AXIS_DIM)
    ang = pos[:, :, None] * inv
    return jnp.cos(ang), jnp.sin(ang)


def _apply_axial_rope(x, cos, sin):
    bsz, t, nh, hd = x.shape
    half = ROPE_AXIS_DIM // 2
    xf = x.astype(F32).reshape(bsz, t, nh, 2, 2, half)
    x1, x2 = xf[..., 0, :], xf[..., 1, :]
    c, s = cos[None, :, None], sin[None, :, None]
    out = jnp.stack([x1 * c - x2 * s, x2 * c + x1 * s], axis=-2)
    return out.reshape(bsz, t, nh, hd).astype(x.dtype)


def _axial_gqa(q, k, v, cos, sin, q_gain, k_gain):
    bsz, t, _ = q.shape
    q = _apply_axial_rope(_rmsnorm(q.reshape(bsz, t, ATT_HEADS, ATT_HD), q_gain), cos, sin)
    k = _apply_axial_rope(_rmsnorm(k.reshape(bsz, t, ATT_KV_HEADS, ATT_HD), k_gain), cos, sin)
    v = v.reshape(bsz, t, ATT_KV_HEADS, ATT_HD)
    nb = t // Q_BLOCK
    qb = jnp.moveaxis(q.reshape(bsz, nb, Q_BLOCK, ATT_KV_HEADS, ATT_GROUP, ATT_HD), 1, 0)
    scale = ATT_HD ** -0.5

    def block(qi):
        s = jnp.einsum('bqkgd,bskd->bkgqs', qi, k).astype(F32) * scale
        p = jax.nn.softmax(s, axis=-1).astype(v.dtype)
        return jnp.einsum('bkgqs,bskd->bqkgd', p, v)

    o = lax.map(block, qb)
    return jnp.moveaxis(o, 0, 1).reshape(bsz, t, ATT_W)


def _mixer_block(u, cos, sin, lb, w_in, hgrn_norm, q_norm, k_norm, w_br_hgrn, w_br_attn, w_out):
    z = u @ w_in
    hq, hf_fwd, hf_bwd, hi, hg, aq, ak, av, gate_h, gate_a = _split_cols(z, IN_SIZES)
    o_h = _hgrn2_bidir(hq, hf_fwd, hf_bwd, hi, hg, lb, hgrn_norm)
    o_a = _axial_gqa(aq, ak, av, cos, sin, q_norm, k_norm)
    merged = jax.nn.sigmoid(gate_h) * (o_h @ w_br_hgrn) + jax.nn.sigmoid(gate_a) * (o_a @ w_br_attn)
    return merged @ w_out


def _cross_attention(u, mem_n, w_cq, w_ckv, w_co):
    bsz, t, _ = u.shape
    m = mem_n.shape[1]
    q = (u @ w_cq).reshape(bsz, t, X_HEADS, X_HD)
    kv = mem_n @ w_ckv
    k = kv[..., :X_W].reshape(bsz, m, X_HEADS, X_HD)
    v = kv[..., X_W:].reshape(bsz, m, X_HEADS, X_HD)
    s = jnp.einsum('bthd,bmhd->bhtm', q, k).astype(F32) * (X_HD ** -0.5)
    p = jax.nn.softmax(s, axis=-1).astype(v.dtype)
    o = jnp.einsum('bhtm,bmhd->bthd', p, v).reshape(bsz, t, X_W)
    return o @ w_co


def _ec_moe(u, w_router, w_gate, w_up, w_down):
    bsz, t, d = u.shape
    n = bsz * t
    cap = max(1, EC_FACTOR * n // N_EXPERTS)
    uf = u.reshape(n, d)
    aff = jax.nn.softmax((uf @ w_router).astype(F32), axis=-1)
    gate, idx = lax.top_k(aff.T, cap)
    xe = uf[idx]
    hid = jax.nn.silu(jnp.einsum('ecd,edf->ecf', xe, w_gate)) * jnp.einsum('ecd,edf->ecf', xe, w_up)
    ye = jnp.einsum('ecf,efd->ecd', hid, w_down)
    contrib = (gate[..., None].astype(ye.dtype) * ye).reshape(-1, d)
    out = jnp.zeros_like(uf).at[idx.reshape(-1)].add(contrib)
    return out.reshape(bsz, t, d)


def _trunk(x, mem, lb_table, norm_mix, w_in, hgrn_norm, q_norm, k_norm, w_br_hgrn, w_br_attn,
           w_out, norm_x, norm_mem, w_cq, w_ckv, w_co, norm_ffn, w_router, w_gate, w_up, w_down,
           norm_final):
    cos, sin = _axial_rope_tables(x.shape[1])
    h = x
    for l in range(DEPTH):
        u = _rmsnorm(h, norm_mix[l])
        h = h + _mixer_block(u, cos, sin, lb_table[:, l], w_in[l], hgrn_norm[l], q_norm[l], k_norm[l],
                             w_br_hgrn[l], w_br_attn[l], w_out[l])
        h = h + _cross_attention(_rmsnorm(h, norm_x[l]), _rmsnorm(mem, norm_mem[l]),
                                 w_cq[l], w_ckv[l], w_co[l])
        h = h + _ec_moe(_rmsnorm(h, norm_ffn[l]), w_router[l], w_gate[l], w_up[l], w_down[l])
    return _rmsnorm(h, norm_final)


def setup_inputs(seed: int = 0) -> dict:
    key = jax.random.key(seed)
    ks = jax.random.split(key, 26)
    L = DEPTH

    def w(k, shape, fan_in):
        return jax.random.normal(k, shape, F32) * (fan_in ** -0.5)

    def gain(k, shape):
        return 1.0 + 0.02 * jax.random.normal(k, shape, F32)

    return {
        'x_prompt': jax.random.normal(ks[0], (BATCH, SEQ, D_MODEL), F32),
        'x_sample': jax.random.normal(ks[1], (DEC_BATCH, DEC_SEQ, D_MODEL), F32),
        'mem_prompt': jax.random.normal(ks[2], (BATCH, MEM_LEN, D_MODEL), F32),
        'mem_sample': jax.random.normal(ks[3], (DEC_BATCH, MEM_LEN, D_MODEL), F32),
        'norm_mix': gain(ks[4], (L, D_MODEL)),
        'w_in': w(ks[5], (L, D_MODEL, IN_WIDTH), D_MODEL),
        'lb_logits': 0.5 * jax.random.normal(ks[6], (2, L + 1, HG_W), F32),
        'hgrn_norm': gain(ks[7], (L, HG_VW)),
        'q_norm': gain(ks[8], (L, ATT_HD)),
        'k_norm': gain(ks[9], (L, ATT_HD)),
        'w_br_hgrn': w(ks[10], (L, HG_VW, D_MODEL), HG_VW),
        'w_br_attn': w(ks[11], (L, ATT_W, D_MODEL), ATT_W),
        'w_out': w(ks[12], (L, D_MODEL, D_MODEL), D_MODEL),
        'norm_x': gain(ks[13], (L, D_MODEL)),
        'norm_mem': gain(ks[14], (L, D_MODEL)),
        'w_cq': w(ks[15], (L, D_MODEL, X_W), D_MODEL),
        'w_ckv': w(ks[16], (L, D_MODEL, 2 * X_W), D_MODEL),
        'w_co': w(ks[17], (L, X_W, D_MODEL), X_W),
        'norm_ffn': gain(ks[18], (L, D_MODEL)),
        'w_router': w(ks[19], (L, D_MODEL, N_EXPERTS), D_MODEL),
        'w_gate': w(ks[20], (L, N_EXPERTS, D_MODEL, D_EXPERT), D_MODEL),
        'w_up': w(ks[21], (L, N_EXPERTS, D_MODEL, D_EXPERT), D_MODEL),
        'w_down': w(ks[22], (L, N_EXPERTS, D_EXPERT, D_MODEL), D_EXPERT),
        'norm_final': gain(ks[23], (D_MODEL,)),
    }


def reference(x_prompt, x_sample, mem_prompt, mem_sample, norm_mix, w_in, lb_logits, hgrn_norm,
              q_norm, k_norm, w_br_hgrn, w_br_attn, w_out, norm_x, norm_mem, w_cq, w_ckv, w_co,
              norm_ffn, w_router, w_gate, w_up, w_down, norm_final):
    lb_table = jnp.cumsum(jax.nn.softmax(lb_logits.astype(F32), axis=1), axis=1)[:, :DEPTH]
    y_prompt = _trunk(x_prompt, mem_prompt, lb_table, norm_mix, w_in, hgrn_norm, q_norm, k_norm,
                      w_br_hgrn, w_br_attn, w_out, norm_x, norm_mem, w_cq, w_ckv, w_co, norm_ffn,
                      w_router, w_gate, w_up, w_down, norm_final)
    y_sample = _trunk(x_sample, mem_sample, lb_table, norm_mix, w_in, hgrn_norm, q_norm, k_norm,
                      w_br_hgrn, w_br_attn, w_out, norm_x, norm_mem, w_cq, w_ckv, w_co, norm_ffn,
                      w_router, w_gate, w_up, w_down, norm_final)
    return (y_prompt, y_sample)
```

```python
import functools

import jax
import jax.numpy as jnp
from jax import lax
from jax.experimental import pallas as pl
from jax.experimental.pallas import tpu as pltpu

F32 = jnp.float32
BF16 = jnp.bfloat16
I32 = jnp.int32

D_MODEL = 1024
EPS = 1e-6
HG_HEADS = 4
HG_D = 128
HG_W = HG_HEADS * HG_D
HG_CHUNK = 64
HG_LEAF = 16
HG_BLOCK = 512
ATT_HEADS = 8
ATT_KV = 2
ATT_GROUP = ATT_HEADS // ATT_KV
ATT_HD = 128
ATT_W = ATT_HEADS * ATT_HD
ATT_KV_W = ATT_KV * ATT_HD
GRID_W = 64
ROPE_HALF = ATT_HD // 4
ROPE_THETA = 10000.0
ATT_TQ = 256
ATT_TK = 512
X_HEADS = 4
X_HD = 128
X_W = X_HEADS * X_HD
N_EXPERTS = 16
EC_FACTOR = 2
D_EXPERT = 2048
ROUTE_BLOCK = 512
ROW_CHUNK = 128
ROW_ALIGN = 16
MAX_ROUTE_BLOCKS = 128
_IN_SIZES = (HG_W, HG_W, HG_W, HG_W, HG_W, ATT_W, ATT_KV_W, ATT_KV_W, D_MODEL, D_MODEL)
_IN_OFF = tuple(sum(_IN_SIZES[:i]) for i in range(len(_IN_SIZES) + 1))
IN_WIDTH = _IN_OFF[-1]

TOKEN_TILE = 256
V7X_VMEM_LIMIT_BYTES = 56 * 1024 * 1024
LOG2E = 1.4426950408889634


def _cparams(sem):
    return pltpu.CompilerParams(dimension_semantics=sem, vmem_limit_bytes=V7X_VMEM_LIMIT_BYTES)


def _const_spec(shape):
    nd = len(shape)
    return pl.BlockSpec(shape, lambda *_: (0,) * nd, pipeline_mode=pl.Buffered(1))


def _rms(x, g):
    return x * lax.rsqrt(jnp.mean(x * x, axis=-1, keepdims=True) + EPS) * g


def _sigmoid(x):
    return 1.0 / (1.0 + jnp.exp(-x))


def _dot(a, b):
    return jnp.dot(a, b, preferred_element_type=F32)


def _dot_nt(a, b):
    return lax.dot_general(a, b, (((1,), (1,)), ((), ())), preferred_element_type=F32)


def _dot_tn(a, b):
    return lax.dot_general(a, b, (((0,), (0,)), ((), ())), preferred_element_type=F32)


def _proj_body(x_ref, nw_ref, w_ref, cos_ref, s1_ref, s2_ref, qg_ref, kg_ref,
               hq_ref, zf_ref, hi_ref, hg_ref, aq_ref, ak_ref, av_ref, gh_ref, ga_ref):
    u = _rms(x_ref[...], nw_ref[...]).astype(BF16)

    def mm(i):
        return _dot(u, w_ref[:, _IN_OFF[i]:_IN_OFF[i + 1]])

    hq_ref[...] = mm(0).astype(BF16)
    zf_ref[0] = mm(1)
    zf_ref[1] = mm(2)
    hi_ref[...] = mm(3).astype(BF16)
    hg_ref[...] = mm(4)
    c, s1, s2 = cos_ref[...], s1_ref[...], s2_ref[...]

    def norm_rope(z, g, scale):
        y = _rms(z, g)
        y = y * c + pltpu.roll(y, ATT_HD - ROPE_HALF, 1) * s1 + pltpu.roll(y, ROPE_HALF, 1) * s2
        return (y * scale).astype(BF16)

    zq = mm(5)
    qscale = (ATT_HD ** -0.5) * LOG2E
    for h in range(ATT_HEADS):
        hs = slice(h * ATT_HD, (h + 1) * ATT_HD)
        aq_ref[:, hs] = norm_rope(zq[:, hs], qg_ref[...], qscale)
    zk = mm(6)
    for h in range(ATT_KV):
        hs = slice(h * ATT_HD, (h + 1) * ATT_HD)
        ak_ref[:, hs] = norm_rope(zk[:, hs], kg_ref[...], 1.0)
    av_ref[...] = mm(7).astype(BF16)
    gh_ref[...] = mm(8)
    ga_ref[...] = mm(9)


def _proj(x, nw, w_in, cos, s1, s2, qg, kg):
    bsz, t, d = x.shape
    tm = TOKEN_TILE
    tok = lambda w: pl.BlockSpec((None, tm, w), lambda b, i: (b, i, 0))
    tab = pl.BlockSpec((tm, ATT_HD), lambda b, i: (i, 0))
    sds = lambda w, dt: jax.ShapeDtypeStruct((bsz, t, w), dt)
    return pl.pallas_call(
        _proj_body,
        grid=(bsz, t // tm),
        in_specs=[tok(d), _const_spec((1, d)), _const_spec((d, IN_WIDTH)), tab, tab, tab,
                  _const_spec((1, ATT_HD)), _const_spec((1, ATT_HD))],
        out_specs=[tok(HG_W),
                   pl.BlockSpec((2, None, tm, HG_W), lambda b, i: (0, b, i, 0)),
                   tok(HG_W), tok(HG_W), tok(ATT_W), tok(ATT_KV_W), tok(ATT_KV_W),
                   tok(D_MODEL), tok(D_MODEL)],
        out_shape=[sds(HG_W, BF16), jax.ShapeDtypeStruct((2, bsz, t, HG_W), F32),
                   sds(HG_W, BF16), sds(HG_W, F32), sds(ATT_W, BF16), sds(ATT_KV_W, BF16),
                   sds(ATT_KV_W, BF16), sds(D_MODEL, F32), sds(D_MODEL, F32)],
        compiler_params=_cparams(("parallel", "parallel")),
        name="proj",
    )(x, nw, w_in, cos, s1, s2, qg, kg)


def _rope_tables(t):
    pos_t = jnp.arange(t, dtype=I32)
    pos = jnp.stack([pos_t // GRID_W, pos_t % GRID_W], axis=-1).astype(F32)
    inv = jnp.power(ROPE_THETA, -jnp.arange(0, 2 * ROPE_HALF, 2, dtype=F32) / (2 * ROPE_HALF))
    ang = pos[:, :, None] * inv
    cos, sin = jnp.cos(ang), jnp.sin(ang)
    zero = jnp.zeros_like(sin)
    c = jnp.concatenate([cos, cos], axis=-1).reshape(t, ATT_HD)
    s1 = jnp.concatenate([-sin, zero], axis=-1).reshape(t, ATT_HD)
    s2 = jnp.concatenate([zero, sin], axis=-1).reshape(t, ATT_HD)
    return c, s1, s2


def _split3(x):
    a = x.astype(BF16)
    r = x - a.astype(F32)
    b = r.astype(BF16)
    c = (r - b.astype(F32)).astype(BF16)
    return a, b, c


def _hgrn_chunk(q, z, v, lb, st, rev):
    C, LS = HG_CHUNK, HG_LEAF
    nb = C // LS
    f = lb + (1.0 - lb) * _sigmoid(z)
    g = jnp.log(f)
    kk = 1.0 - f
    qf = q.astype(F32)
    r_i = lax.broadcasted_iota(I32, (C, C), 0)
    c_i = lax.broadcasted_iota(I32, (C, C), 1)
    before = (c_i >= r_i) if rev else (c_i <= r_i)
    tri = jnp.where(before, 1.0, 0.0).astype(BF16)
    g1, g2, g3 = _split3(g)
    bcum = _dot(tri, g1) + _dot(tri, g2) + _dot(tri, g3)
    btot = bcum[0:1] if rev else bcum[C - 1:C]

    qs = (qf * jnp.exp(bcum)).astype(BF16)
    o = _dot_nt(qs, st.astype(BF16))
    kd = (kk * jnp.exp(btot - bcum)).astype(BF16)
    st_new = st * jnp.exp(btot) + _dot_tn(v, kd)

    cols = range(1, nb) if rev else range(nb - 1)
    lhs, rhs = [], []
    zeros_l = jnp.zeros((LS, HG_D), BF16)
    for p, j in enumerate(cols):
        rr = LS * j if rev else LS * (j + 1) - 1
        ref = bcum[rr:rr + 1]
        lhs.append((qf * jnp.exp(jnp.minimum(bcum - ref, 0.0))).astype(BF16))
        kj = (kk[LS * j:LS * (j + 1)] * jnp.exp(jnp.minimum(ref - bcum[LS * j:LS * (j + 1)], 0.0))).astype(BF16)
        rhs.append((j, jnp.concatenate([kj if i == p else zeros_l for i in range(nb - 1)], axis=1)))
    zero_row = jnp.zeros((LS, HG_D * (nb - 1)), BF16)
    rhs_rows = dict(rhs)
    kmat = jnp.concatenate([rhs_rows.get(j, zero_row) for j in range(nb)], axis=0)
    a_off = _dot_nt(jnp.concatenate(lhs, axis=1), kmat)
    rb, cb = r_i // LS, c_i // LS
    a_off = jnp.where((rb < cb) if rev else (rb > cb), a_off, 0.0)

    lane = lax.broadcasted_iota(I32, (LS, C), 1)
    diag = []
    for i in range(nb):
        sl = slice(LS * i, LS * (i + 1))
        bi, qi, ki = bcum[sl], qf[sl], kk[sl]
        ai = jnp.zeros((LS, C), F32)
        for s in range(LS):
            e = jnp.exp(jnp.minimum(bi - bi[s:s + 1], 0.0))
            col = jnp.sum(qi * e * ki[s:s + 1], axis=1, keepdims=True)
            ai = jnp.where(lane == LS * i + s, col, ai)
        diag.append(ai)
    a_diag = jnp.concatenate(diag, axis=0)
    a_diag = jnp.where(((r_i <= c_i) if rev else (r_i >= c_i)) & (rb == cb), a_diag, 0.0)
    o = o + _dot((a_off + a_diag).astype(BF16), v)
    return o, st_new


def _hgrn_body(q_ref, z_ref, v_ref, lb_ref, o_ref, st_ref):
    d = pl.program_id(1)
    j = pl.program_id(2)
    nch = HG_BLOCK // HG_CHUNK

    @pl.when(j == 0)
    def _():
        st_ref[...] = jnp.zeros_like(st_ref)

    def run(rev):
        def step(i, carry):
            c = (nch - 1 - i) if rev else i
            rows = pl.ds(pl.multiple_of(c * HG_CHUNK, HG_CHUNK), HG_CHUNK)
            for h in range(HG_HEADS):
                hs = slice(h * HG_D, (h + 1) * HG_D)
                o, st = _hgrn_chunk(q_ref[rows, hs], z_ref[rows, hs], v_ref[rows, hs],
                                    lb_ref[int(rev):int(rev) + 1, hs], st_ref[h], rev)
                o_ref[rows, hs] = o
                st_ref[h] = st
            return carry
        lax.fori_loop(0, nch, step, 0)

    @pl.when(d == 0)
    def _():
        run(False)

    @pl.when(d == 1)
    def _():
        run(True)


def _hgrn(hq, zf, hi, lb):
    bsz, t, _ = hq.shape
    nblk = t // HG_BLOCK

    def blk(d, j):
        return jnp.where(d == 0, j, nblk - 1 - j)

    tok = pl.BlockSpec((None, HG_BLOCK, HG_W), lambda b, d, j: (b, blk(d, j), 0))
    dirtok = pl.BlockSpec((None, None, HG_BLOCK, HG_W), lambda b, d, j: (d, b, blk(d, j), 0))
    return pl.pallas_call(
        _hgrn_body,
        grid=(bsz, 2, nblk),
        in_specs=[tok, dirtok, tok, _const_spec((2, HG_W))],
        out_specs=dirtok,
        out_shape=jax.ShapeDtypeStruct((2, bsz, t, HG_W), F32),
        scratch_shapes=[pltpu.VMEM((HG_HEADS, HG_D, HG_D), F32)],
        compiler_params=_cparams(("parallel", "arbitrary", "arbitrary")),
        name="hgrn",
    )(hq, zf, hi, lb)


def _attn_body(q_ref, k_ref, v_ref, o_ref, q4_ref, m_ref, l_ref, acc_ref):
    ki = pl.program_id(3)
    tq = ATT_TQ

    @pl.when(ki == 0)
    def _():
        for g in range(ATT_GROUP):
            q4_ref[g * tq:(g + 1) * tq, :] = q_ref[:, g * ATT_HD:(g + 1) * ATT_HD]
        m_ref[...] = jnp.full_like(m_ref, -jnp.inf)
        l_ref[...] = jnp.zeros_like(l_ref)
        acc_ref[...] = jnp.zeros_like(acc_ref)

    s = _dot_nt(q4_ref[...], k_ref[...])
    m_old = m_ref[...]
    m_new = jnp.maximum(m_old, jnp.max(s, axis=1, keepdims=True))
    alpha = jnp.exp2(m_old - m_new)
    p = jnp.exp2(s - m_new)
    l_ref[...] = alpha * l_ref[...] + jnp.sum(p, axis=1, keepdims=True)
    acc_ref[...] = alpha * acc_ref[...] + _dot(p.astype(BF16), v_ref[...])
    m_ref[...] = m_new

    @pl.when(ki == pl.num_programs(3) - 1)
    def _():
        out = acc_ref[...] / l_ref[...]
        for g in range(ATT_GROUP):
            o_ref[:, g * ATT_HD:(g + 1) * ATT_HD] = out[g * tq:(g + 1) * tq].astype(BF16)


def _attn(aq, ak, av):
    bsz, t, _ = aq.shape
    tq, tk = ATT_TQ, ATT_TK
    gw = ATT_GROUP * ATT_HD
    qspec = pl.BlockSpec((None, tq, gw), lambda b, h, qi, ki: (b, qi, h))
    kspec = pl.BlockSpec((None, tk, ATT_HD), lambda b, h, qi, ki: (b, ki, h))
    return pl.pallas_call(
        _attn_body,
        grid=(bsz, ATT_KV, t // tq, t // tk),
        in_specs=[qspec, kspec, kspec],
        out_specs=qspec,
        out_shape=jax.ShapeDtypeStruct((bsz, t, ATT_W), BF16),
        scratch_shapes=[pltpu.VMEM((ATT_GROUP * tq, ATT_HD), BF16),
                        pltpu.VMEM((ATT_GROUP * tq, 1), F32),
                        pltpu.VMEM((ATT_GROUP * tq, 1), F32),
                        pltpu.VMEM((ATT_GROUP * tq, ATT_HD), F32)],
        compiler_params=_cparams(("parallel", "parallel", "parallel", "arbitrary")),
        name="attn",
    )(aq, ak, av)


def _memkv_body(m_ref, nw_ref, w_ref, o_ref):
    o_ref[...] = _dot(_rms(m_ref[...], nw_ref[...]).astype(BF16), w_ref[...]).astype(BF16)


def _memkv(mem, nw, w_ckv):
    bsz, m, d = mem.shape
    return pl.pallas_call(
        _memkv_body,
        grid=(bsz,),
        in_specs=[pl.BlockSpec((None, m, d), lambda b: (b, 0, 0)), _const_spec((1, d)),
                  _const_spec((d, 2 * X_W))],
        out_specs=pl.BlockSpec((None, m, 2 * X_W), lambda b: (b, 0, 0)),
        out_shape=jax.ShapeDtypeStruct((bsz, m, 2 * X_W), BF16),
        compiler_params=_cparams(("parallel",)),
        name="memkv",
    )(mem, nw, w_ckv)


def _post_body(x_ref, o2_ref, hg_ref, oa_ref, gh_ref, ga_ref, kv_ref,
               hgn_ref, wbh_ref, wba_ref, wo_ref, nx_ref, wcq_ref, wco_ref, nf_ref, wr_ref,
               h2_ref, u3_ref, aff_ref):
    o = o2_ref[0] + o2_ref[1]
    parts = []
    for h in range(HG_HEADS):
        oh = o[:, h * HG_D:(h + 1) * HG_D]
        parts.append(oh * lax.rsqrt(jnp.mean(oh * oh, axis=-1, keepdims=True) + EPS))
    hgv = hg_ref[...]
    oh = (jnp.concatenate(parts, axis=1) * hgn_ref[...] * (hgv * _sigmoid(hgv))).astype(BF16)
    merged = (_sigmoid(gh_ref[...]) * _dot(oh, wbh_ref[...])
              + _sigmoid(ga_ref[...]) * _dot(oa_ref[...], wba_ref[...]))
    h1 = x_ref[...] + _dot(merged.astype(BF16), wo_ref[...])

    u2 = _rms(h1, nx_ref[...]).astype(BF16)
    qx = (_dot(u2, wcq_ref[...]) * (X_HD ** -0.5)).astype(BF16)
    outs = []
    for h in range(X_HEADS):
        hs = slice(h * X_HD, (h + 1) * X_HD)
        s = _dot_nt(qx[:, hs], kv_ref[:, hs])
        p = jnp.exp(s - jnp.max(s, axis=1, keepdims=True))
        ov = _dot(p.astype(BF16), kv_ref[:, X_W + h * X_HD:X_W + (h + 1) * X_HD])
        outs.append(ov / jnp.sum(p, axis=1, keepdims=True))
    h2 = h1 + _dot(jnp.concatenate(outs, axis=1).astype(BF16), wco_ref[...])
    h2_ref[...] = h2

    u3 = _rms(h2, nf_ref[...]).astype(BF16)
    u3_ref[...] = u3
    logits = _dot_nt(wr_ref[...], u3)
    e = jnp.exp(logits - jnp.max(logits, axis=0, keepdims=True))
    aff_ref[...] = e / jnp.sum(e, axis=0, keepdims=True)


def _post(x, o2, hg, oa, gh, ga, kv, hgn, wbh, wba, wo, nx, wcq, wco, nf, wr_t):
    bsz, t, d = x.shape
    tm = TOKEN_TILE
    nt = t // tm
    m = kv.shape[1]
    tok = lambda w: pl.BlockSpec((None, tm, w), lambda b, i: (b, i, 0))
    return pl.pallas_call(
        _post_body,
        grid=(bsz, nt),
        in_specs=[tok(d), pl.BlockSpec((2, None, tm, HG_W), lambda b, i: (0, b, i, 0)), tok(HG_W),
                  tok(ATT_W), tok(d), tok(d), pl.BlockSpec((None, m, 2 * X_W), lambda b, i: (b, 0, 0)),
                  _const_spec((1, HG_W)), _const_spec((HG_W, d)), _const_spec((ATT_W, d)),
                  _const_spec((d, d)), _const_spec((1, d)), _const_spec((d, X_W)), _const_spec((X_W, d)),
                  _const_spec((1, d)), _const_spec((N_EXPERTS, d))],
        out_specs=[tok(d), tok(d), pl.BlockSpec((N_EXPERTS, tm), lambda b, i: (0, b * nt + i))],
        out_shape=[jax.ShapeDtypeStruct((bsz, t, d), F32), jax.ShapeDtypeStruct((bsz, t, d), BF16),
                   jax.ShapeDtypeStruct((N_EXPERTS, bsz * t), F32)],
        compiler_params=_cparams(("parallel", "parallel")),
        name="post",
    )(x, o2, hg, oa, gh, ga, kv, hgn, wbh, wba, wo, nx, wcq, wco, nf, wr_t)


def _route_body(aff_ref, pos_ref, off_ref, cnt_ref, *, cap):
    n = aff_ref.shape[1]
    nblk = n // ROUTE_BLOCK
    bits = pltpu.bitcast(aff_ref[...], I32)

    def count_ge(th):
        return jnp.sum(jnp.where(bits >= th, 1.0, 0.0), axis=1, keepdims=True)

    def bisect(_, lohi):
        lo, hi = lohi
        mid = lo + lax.shift_right_logical(hi - lo, 1)
        ok = count_ge(mid) >= cap
        return jnp.where(ok, mid, lo), jnp.where(ok, hi, mid)

    lo0 = jnp.zeros((N_EXPERTS, 1), I32)
    hi0 = jnp.full((N_EXPERTS, 1), 0x7F800000, I32)
    thr, _ = lax.fori_loop(0, 31, bisect, (lo0, hi0))
    n_gt = jnp.sum(jnp.where(bits > thr, 1.0, 0.0), axis=1, keepdims=True)
    need = cap - n_gt

    r_i = lax.broadcasted_iota(I32, (ROUTE_BLOCK, ROUTE_BLOCK), 0)
    c_i = lax.broadcasted_iota(I32, (ROUTE_BLOCK, ROUTE_BLOCK), 1)
    tri = jnp.where(r_i < c_i, 1.0, 0.0).astype(BF16)
    lane = lax.broadcasted_iota(I32, (N_EXPERTS, MAX_ROUTE_BLOCKS), 1)

    off_ref[...] = jnp.zeros_like(off_ref)
    cnt_ref[...] = jnp.zeros_like(cnt_ref)

    def scan(j, carry):
        run_sel, run_eq = carry
        cols = pl.ds(pl.multiple_of(j * ROUTE_BLOCK, ROUTE_BLOCK), ROUTE_BLOCK)
        b = pltpu.bitcast(aff_ref[:, cols], I32)
        eq = jnp.where(b == thr, 1.0, 0.0)
        eq_rank = run_eq + _dot(eq.astype(BF16), tri)
        sel = jnp.where(b > thr, 1.0, jnp.where(eq_rank < need, eq, 0.0))
        pos = run_sel + _dot(sel.astype(BF16), tri)
        pos_ref[:, cols] = jnp.where(sel > 0.0, pos, -1.0).astype(I32)
        c = jnp.sum(sel, axis=1, keepdims=True)
        off_ref[...] = jnp.where(lane == j, run_sel.astype(I32), off_ref[...])
        cnt_ref[...] = jnp.where(lane == j, c.astype(I32), cnt_ref[...])
        return run_sel + c, run_eq + jnp.sum(eq, axis=1, keepdims=True)

    z1 = n_gt * 0.0
    lax.fori_loop(0, nblk, scan, (z1, z1))


def _route(aff_t, cap):
    n = aff_t.shape[1]
    assert n % ROUTE_BLOCK == 0 and n // ROUTE_BLOCK <= MAX_ROUTE_BLOCKS
    tbl = jax.ShapeDtypeStruct((N_EXPERTS, MAX_ROUTE_BLOCKS), I32)
    return pl.pallas_call(
        functools.partial(_route_body, cap=cap),
        out_shape=[jax.ShapeDtypeStruct((N_EXPERTS, n), I32), tbl, tbl],
        compiler_params=pltpu.CompilerParams(vmem_limit_bytes=V7X_VMEM_LIMIT_BYTES),
        name="route",
    )(aff_t)


_STAGE_SLOTS = 8


def _dispatch_body(off_s, cnt_s, pos_ref, u_ref, xe_ref, stage_ref, tail_ref, sem_ref, ctr_ref, *, cap_pad):
    j = pl.program_id(0)
    last = pl.num_programs(0) - 1

    @pl.when(j == 0)
    def _():
        tail_ref[...] = jnp.zeros_like(tail_ref)
        ctr_ref[0] = 0
        ctr_ref[1] = 0

    def copy(slot, row):
        return pltpu.make_async_copy(stage_ref.at[slot], xe_ref.at[pl.ds(row, ROW_CHUNK), :], sem_ref.at[slot])

    def wait_oldest():
        copy(ctr_ref[1] % _STAGE_SLOTS, 0).wait()
        ctr_ref[1] = ctr_ref[1] + 1

    def drain():
        lax.fori_loop(ctr_ref[1], ctr_ref[0], lambda _, c: (wait_oldest(), c)[1], 0)

    drain()
    u = u_ref[...]
    r_i = lax.broadcasted_iota(I32, (ROW_CHUNK, ROUTE_BLOCK), 0)

    def expert(e, carry):
        off = off_s[e * MAX_ROUTE_BLOCKS + j]
        c = cnt_s[e * MAX_ROUTE_BLOCKS + j]
        a = off & (ROW_ALIGN - 1)
        base = off - a
        nch = jnp.where(c > 0, (a + c + ROW_CHUNK - 1) // ROW_CHUNK, 0)
        pos = pos_ref[pl.ds(e, 1), :]

        def chunk(ch, carry):
            onehot = jnp.where(r_i == pos - (base + ch * ROW_CHUNK), 1.0, 0.0).astype(BF16)
            rows = _dot(onehot, u)
            carried = jnp.where(ch == 0, tail_ref[e].astype(F32), 0.0)
            rows = jnp.concatenate([rows[:ROW_ALIGN] + carried, rows[ROW_ALIGN:]], axis=0)

            @pl.when(ctr_ref[0] - ctr_ref[1] >= _STAGE_SLOTS)
            def _():
                wait_oldest()

            slot = ctr_ref[0] % _STAGE_SLOTS
            stage_ref[slot] = rows.astype(BF16)
            row0 = pl.multiple_of(e * cap_pad + base + ch * ROW_CHUNK, ROW_ALIGN)
            copy(slot, row0).start()
            ctr_ref[0] = ctr_ref[0] + 1
            return carry

        lax.fori_loop(0, nch, chunk, 0)

        @pl.when(c > 0)
        def _():
            end = a + c
            part = end & (ROW_ALIGN - 1)
            slot = (ctr_ref[0] - 1) % _STAGE_SLOTS
            r0 = pl.multiple_of((end - part) & (ROW_CHUNK - 1), ROW_ALIGN)
            keep = stage_ref[slot, pl.ds(r0, ROW_ALIGN), :]
            tail_ref[e] = jnp.where(part > 0, keep, jnp.zeros_like(keep))

        return carry

    lax.fori_loop(0, N_EXPERTS, expert, 0)

    @pl.when(j == last)
    def _():
        drain()
        stage_ref[0] = jnp.zeros((ROW_CHUNK, stage_ref.shape[2]), BF16)
        pads = [copy(0, e * cap_pad + cap_pad - ROW_CHUNK) for e in range(N_EXPERTS)]
        for cp in pads:
            cp.start()
        for cp in pads:
            cp.wait()


def _dispatch(off, cnt, pos_t, u, cap):
    n, d = u.shape
    cap_pad = cap + ROW_CHUNK
    nblk = n // ROUTE_BLOCK
    return pl.pallas_call(
        functools.partial(_dispatch_body, cap_pad=cap_pad),
        grid_spec=pltpu.PrefetchScalarGridSpec(
            num_scalar_prefetch=2,
            grid=(nblk,),
            in_specs=[pl.BlockSpec((N_EXPERTS, ROUTE_BLOCK), lambda j, *_: (0, j)),
                      pl.BlockSpec((ROUTE_BLOCK, d), lambda j, *_: (j, 0))],
            out_specs=pl.BlockSpec(memory_space=pl.ANY),
            scratch_shapes=[pltpu.VMEM((_STAGE_SLOTS, ROW_CHUNK, d), BF16),
                            pltpu.VMEM((N_EXPERTS, ROW_ALIGN, d), BF16),
                            pltpu.SemaphoreType.DMA((_STAGE_SLOTS,)),
                            pltpu.SMEM((2,), I32)]),
        out_shape=jax.ShapeDtypeStruct((N_EXPERTS * cap_pad, d), BF16),
        compiler_params=_cparams(("arbitrary",)),
        name="dispatch",
    )(off, cnt, pos_t, u)


_FFN_COLS = 512


def _ffn_body(x_ref, wg_ref, wu_ref, wd_ref, o_ref):
    x = x_ref[...]
    acc = None
    for c in range(D_EXPERT // _FFN_COLS):
        cs = slice(c * _FFN_COLS, (c + 1) * _FFN_COLS)
        g = _dot(x, wg_ref[:, cs])
        hid = (g * _sigmoid(g) * _dot(x, wu_ref[:, cs])).astype(BF16)
        part = _dot(hid, wd_ref[cs, :])
        acc = part if acc is None else acc + part
    o_ref[...] = acc.astype(BF16)


def _ffn(xe, wg, wu, wd, cap):
    e, _, d = xe.shape
    tm = 512 if cap % 512 == 0 else ROW_CHUNK
    wspec = lambda a, b: pl.BlockSpec((None, a, b), lambda ei, i: (ei, 0, 0))
    return pl.pallas_call(
        _ffn_body,
        grid=(e, cap // tm),
        in_specs=[pl.BlockSpec((None, tm, d), lambda ei, i: (ei, i, 0)),
                  wspec(d, D_EXPERT), wspec(d, D_EXPERT), wspec(D_EXPERT, d)],
        out_specs=pl.BlockSpec((None, tm, d), lambda ei, i: (ei, i, 0)),
        out_shape=jax.ShapeDtypeStruct((e, cap, d), BF16),
        compiler_params=_cparams(("parallel", "parallel")),
        name="ffn",
    )(xe, wg, wu, wd)


def _combine_body(off_s, cnt_s, pos_ref, aff_ref, h_ref, nw_ref, ye_ref, y_ref, buf_ref, xbuf_ref, sem_ref,
                  xsem_ref, acc_ref, *, cap):
    j = pl.program_id(0)

    def geom(e):
        off = off_s[e * MAX_ROUTE_BLOCKS + j]
        c = cnt_s[e * MAX_ROUTE_BLOCKS + j]
        a = off & (ROW_ALIGN - 1)
        return off - a, c, jnp.where(c > 0, (a + c + ROW_CHUNK - 1) // ROW_CHUNK, 0)

    def window(e, base, ch):
        lo = base + ch * ROW_CHUNK
        start = pl.multiple_of(jnp.minimum(lo, cap - ROW_CHUNK), ROW_ALIGN)
        return lo, start

    def first_copy(e, start):
        return pltpu.make_async_copy(ye_ref.at[pl.ds(pl.multiple_of(e * cap + start, ROW_ALIGN), ROW_CHUNK), :],
                                     buf_ref.at[e], sem_ref.at[e])

    for e in range(N_EXPERTS):
        base, c, _ = geom(e)
        _, start = window(e, base, 0)

        @pl.when(c > 0)
        def _():
            first_copy(e, start).start()

    acc_ref[...] = jnp.zeros_like(acc_ref)
    r_i = lax.broadcasted_iota(I32, (ROW_CHUNK, ROUTE_BLOCK), 0)

    def expand(e, lo, start, rows):
        pos = pos_ref[e:e + 1, :]
        gate = aff_ref[e:e + 1, :]
        hit = (r_i == pos - start) & (pos >= lo) & (pos < lo + ROW_CHUNK)
        g_hi = gate.astype(BF16).astype(F32)
        g_lo = gate - g_hi
        acc_ref[...] += (_dot_tn(jnp.where(hit, g_hi, 0.0).astype(BF16), rows)
                         + _dot_tn(jnp.where(hit, g_lo, 0.0).astype(BF16), rows))

    for e in range(N_EXPERTS):
        base, c, nch = geom(e)
        lo, start = window(e, base, 0)

        @pl.when(c > 0)
        def _():
            first_copy(e, start).wait()
            expand(e, lo, start, buf_ref[e])

        def extra(ch, carry):
            lo_c, start_c = window(e, base, ch)
            cp = pltpu.make_async_copy(
                ye_ref.at[pl.ds(pl.multiple_of(e * cap + start_c, ROW_ALIGN), ROW_CHUNK), :], xbuf_ref, xsem_ref)
            cp.start()
            cp.wait()
            expand(e, lo_c, start_c, xbuf_ref[...])
            return carry

        lax.fori_loop(1, jnp.maximum(nch, 1), extra, 0)

    y_ref[...] = _rms(h_ref[...] + acc_ref[...], nw_ref[...])


def _combine(off, cnt, pos_t, aff_t, h2, nw, ye, cap):
    n, d = h2.shape
    nblk = n // ROUTE_BLOCK
    eblk = pl.BlockSpec((N_EXPERTS, ROUTE_BLOCK), lambda j, *_: (0, j))
    tok = pl.BlockSpec((ROUTE_BLOCK, d), lambda j, *_: (j, 0))
    return pl.pallas_call(
        functools.partial(_combine_body, cap=cap),
        grid_spec=pltpu.PrefetchScalarGridSpec(
            num_scalar_prefetch=2,
            grid=(nblk,),
            in_specs=[eblk, eblk, tok, pl.BlockSpec((1, d), lambda j, *_: (0, 0)),
                      pl.BlockSpec(memory_space=pl.ANY)],
            out_specs=tok,
            scratch_shapes=[pltpu.VMEM((N_EXPERTS, ROW_CHUNK, d), BF16),
                            pltpu.VMEM((ROW_CHUNK, d), BF16),
                            pltpu.SemaphoreType.DMA((N_EXPERTS,)),
                            pltpu.SemaphoreType.DMA(()),
                            pltpu.VMEM((ROUTE_BLOCK, d), F32)]),
        out_shape=jax.ShapeDtypeStruct((n, d), F32),
        compiler_params=_cparams(("arbitrary",)),
        name="combine",
    )(off, cnt, pos_t, aff_t, h2, nw, ye)


def _trunk(x, mem, lb, p):
    bsz, t, d = x.shape
    n = bsz * t
    cap = max(1, EC_FACTOR * n // N_EXPERTS)
    assert cap % ROW_CHUNK == 0 and t % HG_BLOCK == 0 and t % ATT_TK == 0
    cos, s1, s2 = _rope_tables(t)
    hq, zf, hi, hg, aq, ak, av, gh, ga = _proj(x, p["norm_mix"], p["w_in"], cos, s1, s2, p["q_norm"], p["k_norm"])
    o2 = _hgrn(hq, zf, hi, lb)
    oa = _attn(aq, ak, av)
    kv = _memkv(mem, p["norm_mem"], p["w_ckv"])
    h2, u3, aff_t = _post(x, o2, hg, oa, gh, ga, kv, p["hgrn_norm"], p["w_br_hgrn"], p["w_br_attn"], p["w_out"],
                          p["norm_x"], p["w_cq"], p["w_co"], p["norm_ffn"], p["w_router_t"])
    pos_t, off, cnt = _route(aff_t, cap)
    off, cnt = off.reshape(-1), cnt.reshape(-1)
    xe = _dispatch(off, cnt, pos_t, u3.reshape(n, d), cap)
    ye = _ffn(xe.reshape(N_EXPERTS, cap + ROW_CHUNK, d), p["w_gate"], p["w_up"], p["w_down"], cap)
    y = _combine(off, cnt, pos_t, aff_t, h2.reshape(n, d), p["norm_final"], ye.reshape(N_EXPERTS * cap, d), cap)
    return y.reshape(bsz, t, d)


def kernel(x_prompt, x_sample, mem_prompt, mem_sample, norm_mix, w_in, lb_logits, hgrn_norm, q_norm, k_norm,
           w_br_hgrn, w_br_attn, w_out, norm_x, norm_mem, w_cq, w_ckv, w_co, norm_ffn, w_router, w_gate, w_up,
           w_down, norm_final):
    row = lambda v: v.reshape(1, -1).astype(F32)
    p = {
        "norm_mix": row(norm_mix[0]), "w_in": w_in[0].astype(BF16),
        "hgrn_norm": row(hgrn_norm[0]), "q_norm": row(q_norm[0]), "k_norm": row(k_norm[0]),
        "w_br_hgrn": w_br_hgrn[0].astype(BF16), "w_br_attn": w_br_attn[0].astype(BF16),
        "w_out": w_out[0].astype(BF16), "norm_x": row(norm_x[0]), "norm_mem": row(norm_mem[0]),
        "w_cq": w_cq[0].astype(BF16), "w_ckv": w_ckv[0].astype(BF16), "w_co": w_co[0].astype(BF16),
        "norm_ffn": row(norm_ffn[0]), "w_router_t": w_router[0].T.astype(BF16),
        "w_gate": w_gate[0].astype(BF16), "w_up": w_up[0].astype(BF16), "w_down": w_down[0].astype(BF16),
        "norm_final": row(norm_final),
    }
    lb = jnp.cumsum(jax.nn.softmax(lb_logits.astype(F32), axis=1), axis=1)[:, 0]
    return (_trunk(x_prompt, mem_prompt, lb, p), _trunk(x_sample, mem_sample, lb, p))
```

```python
import functools

import jax
import jax.numpy as jnp
from jax import lax
from jax.experimental import pallas as pl
from jax.experimental.pallas import tpu as pltpu

F32 = jnp.float32
BF16 = jnp.bfloat16
I32 = jnp.int32

D_MODEL = 1024
EPS = 1e-6
HG_HEADS = 4
HG_D = 128
HG_W = HG_HEADS * HG_D
HG_CHUNK = 64
HG_LEAF = 16
HG_BLOCK = 512
ATT_HEADS = 8
ATT_KV = 2
ATT_GROUP = ATT_HEADS // ATT_KV
ATT_HD = 128
ATT_W = ATT_HEADS * ATT_HD
ATT_KV_W = ATT_KV * ATT_HD
GRID_W = 64
ROPE_HALF = ATT_HD // 4
ROPE_THETA = 10000.0
ATT_TQ = 256
ATT_TKS = 512
ATT_UNROLL = 4
X_HEADS = 4
X_HD = 128
X_W = X_HEADS * X_HD
N_EXPERTS = 16
EC_FACTOR = 2
D_EXPERT = 2048
ROUTE_BLOCK = 512
ROW_CHUNK = 128
ROW_ALIGN = 16
MAX_ROUTE_BLOCKS = 128
_IN_SIZES = (HG_W, HG_W, HG_W, HG_W, HG_W, ATT_W, ATT_KV_W, ATT_KV_W, D_MODEL, D_MODEL)
_IN_OFF = tuple(sum(_IN_SIZES[:i]) for i in range(len(_IN_SIZES) + 1))
IN_WIDTH = _IN_OFF[-1]

TOKEN_TILE = 256
V7X_VMEM_LIMIT_BYTES = 56 * 1024 * 1024
LOG2E = 1.4426950408889634


def _cparams(sem):
    return pltpu.CompilerParams(dimension_semantics=sem, vmem_limit_bytes=V7X_VMEM_LIMIT_BYTES)


def _const_spec(shape):
    nd = len(shape)
    return pl.BlockSpec(shape, lambda *_: (0,) * nd, pipeline_mode=pl.Buffered(1))


def _rms(x, g):
    return x * lax.rsqrt(jnp.mean(x * x, axis=-1, keepdims=True) + EPS) * g


def _sigmoid(x):
    return 1.0 / (1.0 + jnp.exp(-x))


def _dot(a, b):
    return jnp.dot(a, b, preferred_element_type=F32)


def _dot_nt(a, b):
    return lax.dot_general(a, b, (((1,), (1,)), ((), ())), preferred_element_type=F32)


def _dot_tn(a, b):
    return lax.dot_general(a, b, (((0,), (0,)), ((), ())), preferred_element_type=F32)


def _proj_body(x_ref, nw_ref, w_ref, cos_ref, s1_ref, s2_ref, qg_ref, kg_ref,
               hq_ref, zf_ref, hi_ref, hg_ref, aq_ref, ak_ref, av_ref, gh_ref, ga_ref):
    u = _rms(x_ref[...], nw_ref[...]).astype(BF16)

    def mm(i):
        return _dot(u, w_ref[:, _IN_OFF[i]:_IN_OFF[i + 1]])

    hq_ref[...] = mm(0).astype(BF16)
    zf_ref[0] = mm(1)
    zf_ref[1] = mm(2)
    hi_ref[...] = mm(3).astype(BF16)
    hg_ref[...] = mm(4)
    c, s1, s2 = cos_ref[...], s1_ref[...], s2_ref[...]

    def norm_rope(z, g, scale):
        y = _rms(z, g)
        y = y * c + pltpu.roll(y, ATT_HD - ROPE_HALF, 1) * s1 + pltpu.roll(y, ROPE_HALF, 1) * s2
        return y * scale

    zq = mm(5)
    qscale = (ATT_HD ** -0.5) * LOG2E
    for h in range(ATT_HEADS):
        hs = slice(h * ATT_HD, (h + 1) * ATT_HD)
        aq_ref[h] = norm_rope(zq[:, hs], qg_ref[...], qscale).T.astype(BF16)
    zk = mm(6)
    for h in range(ATT_KV):
        hs = slice(h * ATT_HD, (h + 1) * ATT_HD)
        ak_ref[:, hs] = norm_rope(zk[:, hs], kg_ref[...], 1.0).astype(BF16)
    zv = mm(7)
    for h in range(ATT_KV):
        av_ref[h] = zv[:, h * ATT_HD:(h + 1) * ATT_HD].T.astype(BF16)
    gh_ref[...] = mm(8)
    ga_ref[...] = mm(9)


def _proj(x, nw, w_in, cos, s1, s2, qg, kg):
    bsz, t, d = x.shape
    tm = TOKEN_TILE
    tok = lambda w: pl.BlockSpec((None, tm, w), lambda b, i: (b, i, 0))
    tab = pl.BlockSpec((tm, ATT_HD), lambda b, i: (i, 0))
    sds = lambda w, dt: jax.ShapeDtypeStruct((bsz, t, w), dt)
    head_t = lambda nh: pl.BlockSpec((None, nh, ATT_HD, tm), lambda b, i: (b, 0, 0, i))
    sds_t = lambda nh: jax.ShapeDtypeStruct((bsz, nh, ATT_HD, t), BF16)
    return pl.pallas_call(
        _proj_body,
        grid=(bsz, t // tm),
        in_specs=[tok(d), _const_spec((1, d)), _const_spec((d, IN_WIDTH)), tab, tab, tab,
                  _const_spec((1, ATT_HD)), _const_spec((1, ATT_HD))],
        out_specs=[tok(HG_W),
                   pl.BlockSpec((2, None, tm, HG_W), lambda b, i: (0, b, i, 0)),
                   tok(HG_W), tok(HG_W), head_t(ATT_HEADS), tok(ATT_KV_W), head_t(ATT_KV),
                   tok(D_MODEL), tok(D_MODEL)],
        out_shape=[sds(HG_W, BF16), jax.ShapeDtypeStruct((2, bsz, t, HG_W), F32),
                   sds(HG_W, BF16), sds(HG_W, F32), sds_t(ATT_HEADS), sds(ATT_KV_W, BF16),
                   sds_t(ATT_KV), sds(D_MODEL, F32), sds(D_MODEL, F32)],
        compiler_params=_cparams(("parallel", "parallel")),
        name="proj",
    )(x, nw, w_in, cos, s1, s2, qg, kg)


def _rope_tables(t):
    pos_t = jnp.arange(t, dtype=I32)
    pos = jnp.stack([pos_t // GRID_W, pos_t % GRID_W], axis=-1).astype(F32)
    inv = jnp.power(ROPE_THETA, -jnp.arange(0, 2 * ROPE_HALF, 2, dtype=F32) / (2 * ROPE_HALF))
    ang = pos[:, :, None] * inv
    cos, sin = jnp.cos(ang), jnp.sin(ang)
    zero = jnp.zeros_like(sin)
    c = jnp.concatenate([cos, cos], axis=-1).reshape(t, ATT_HD)
    s1 = jnp.concatenate([-sin, zero], axis=-1).reshape(t, ATT_HD)
    s2 = jnp.concatenate([zero, sin], axis=-1).reshape(t, ATT_HD)
    return c, s1, s2


def _split3(x):
    a = x.astype(BF16)
    r = x - a.astype(F32)
    b = r.astype(BF16)
    c = (r - b.astype(F32)).astype(BF16)
    return a, b, c


def _hgrn_chunk(q, z, v, lb, st, rev):
    C, LS = HG_CHUNK, HG_LEAF
    nb = C // LS
    f = lb + (1.0 - lb) * _sigmoid(z)
    g = jnp.log(f)
    kk = 1.0 - f
    qf = q.astype(F32)
    r_i = lax.broadcasted_iota(I32, (C, C), 0)
    c_i = lax.broadcasted_iota(I32, (C, C), 1)
    before = (c_i >= r_i) if rev else (c_i <= r_i)
    tri = jnp.where(before, 1.0, 0.0).astype(BF16)
    g1, g2, g3 = _split3(g)
    b2 = (_dot(tri, g1) + _dot(tri, g2) + _dot(tri, g3)) * LOG2E
    btot = b2[0:1] if rev else b2[C - 1:C]

    qs = (qf * jnp.exp2(b2)).astype(BF16)
    o = _dot_nt(qs, st.astype(BF16))
    kd = (kk * jnp.exp2(btot - b2)).astype(BF16)
    st_new = st * jnp.exp2(btot) + _dot_tn(v, kd)

    cols = range(1, nb) if rev else range(nb - 1)
    lhs, rhs_rows = [], {}
    zeros_l = jnp.zeros((LS, HG_D), BF16)
    for p, j in enumerate(cols):
        rr = LS * j if rev else LS * (j + 1) - 1
        ref = b2[rr:rr + 1]
        rows = slice(0, LS * j) if rev else slice(LS * (j + 1), C)
        scaled = (qf[rows] * jnp.exp2(b2[rows] - ref)).astype(BF16)
        pad = jnp.zeros((C - scaled.shape[0], HG_D), BF16)
        lhs.append(jnp.concatenate([scaled, pad] if rev else [pad, scaled], axis=0))
        kj = (kk[LS * j:LS * (j + 1)] * jnp.exp2(ref - b2[LS * j:LS * (j + 1)])).astype(BF16)
        rhs_rows[j] = jnp.concatenate([kj if i == p else zeros_l for i in range(nb - 1)], axis=1)
    zero_row = jnp.zeros((LS, HG_D * (nb - 1)), BF16)
    kmat = jnp.concatenate([rhs_rows.get(j, zero_row) for j in range(nb)], axis=0)
    a_off = _dot_nt(jnp.concatenate(lhs, axis=1), kmat)

    SUB = 8
    lane = lax.broadcasted_iota(I32, (SUB, C), 1)
    tiles = [jnp.zeros((SUB, C), F32) for _ in range(C // SUB)]
    for s in range(C):
        blk = s // LS
        first, last = (LS * blk // SUB, s // SUB) if rev else (s // SUB, LS * (blk + 1) // SUB - 1)
        brow, krow = b2[s:s + 1], kk[s:s + 1]
        for t in range(first, last + 1):
            rs = slice(SUB * t, SUB * (t + 1))
            d = b2[rs] - brow
            if t == s // SUB:
                d = jnp.minimum(d, 0.0)
            col = jnp.sum(qf[rs] * jnp.exp2(d) * krow, axis=1, keepdims=True)
            tiles[t] = jnp.where(lane == s, col, tiles[t])
    a_diag = jnp.where((r_i <= c_i) if rev else (r_i >= c_i), jnp.concatenate(tiles, axis=0), 0.0)
    o = o + _dot((a_off + a_diag).astype(BF16), v)
    return o, st_new


def _hgrn_body(q_ref, z_ref, v_ref, lb_ref, o_ref, st_ref):
    d = pl.program_id(1)
    j = pl.program_id(2)
    nch = HG_BLOCK // HG_CHUNK

    @pl.when(j == 0)
    def _():
        st_ref[...] = jnp.zeros_like(st_ref)

    def run(rev):
        def step(i, carry):
            c = (nch - 1 - i) if rev else i
            rows = pl.ds(pl.multiple_of(c * HG_CHUNK, HG_CHUNK), HG_CHUNK)
            for h in range(HG_HEADS):
                hs = slice(h * HG_D, (h + 1) * HG_D)
                o, st = _hgrn_chunk(q_ref[rows, hs], z_ref[rows, hs], v_ref[rows, hs],
                                    lb_ref[int(rev):int(rev) + 1, hs], st_ref[h], rev)
                o_ref[rows, hs] = o
                st_ref[h] = st
            return carry
        lax.fori_loop(0, nch, step, 0)

    @pl.when(d == 0)
    def _():
        run(False)

    @pl.when(d == 1)
    def _():
        run(True)


def _hgrn(hq, zf, hi, lb):
    bsz, t, _ = hq.shape
    nblk = t // HG_BLOCK

    def blk(d, j):
        return jnp.where(d == 0, j, nblk - 1 - j)

    tok = pl.BlockSpec((None, HG_BLOCK, HG_W), lambda b, d, j: (b, blk(d, j), 0))
    dirtok = pl.BlockSpec((None, None, HG_BLOCK, HG_W), lambda b, d, j: (d, b, blk(d, j), 0))
    return pl.pallas_call(
        _hgrn_body,
        grid=(bsz, 2, nblk),
        in_specs=[tok, dirtok, tok, _const_spec((2, HG_W))],
        out_specs=dirtok,
        out_shape=jax.ShapeDtypeStruct((2, bsz, t, HG_W), F32),
        scratch_shapes=[pltpu.VMEM((HG_HEADS, HG_D, HG_D), F32)],
        compiler_params=_cparams(("parallel", "arbitrary", "arbitrary")),
        name="hgrn",
    )(hq, zf, hi, lb)


def _attn_body(q_ref, k_ref, v_ref, o_ref, qt_ref, s0_ref, m_ref, l_ref, acc_ref, *, unroll):
    tq, tks = ATT_TQ, ATT_TKS
    nsub = k_ref.shape[0] // tks
    for g in range(ATT_GROUP):
        qt_ref[:, g * tq:(g + 1) * tq] = q_ref[g]
    m_ref[...] = jnp.full_like(m_ref, -jnp.inf)
    l_ref[...] = jnp.zeros_like(l_ref)
    acc_ref[...] = jnp.zeros_like(acc_ref)
    qt = qt_ref[...]
    s0_ref[...] = _dot(k_ref[0:tks, :], qt)

    def group(i, carry):
        m, l, acc = m_ref[...], l_ref[...], acc_ref[...]
        s_next = s0_ref[...]
        for u in range(unroll):
            c = i * unroll + u
            s = s_next
            nxt = jnp.minimum(c + 1, nsub - 1)
            s_next = _dot(k_ref[pl.ds(pl.multiple_of(nxt * tks, tks), tks), :], qt)
            m_new = jnp.maximum(m, jnp.max(s, axis=0, keepdims=True))
            alpha = jnp.exp2(m - m_new)
            p = jnp.exp2(s - m_new)
            l = alpha * l + jnp.sum(p, axis=0, keepdims=True)
            acc = alpha * acc + _dot(v_ref[:, pl.ds(pl.multiple_of(c * tks, tks), tks)], p.astype(BF16))
            m = m_new
        s0_ref[...] = s_next
        m_ref[...], l_ref[...], acc_ref[...] = m, l, acc
        return carry

    lax.fori_loop(0, nsub // unroll, group, 0)
    out = acc_ref[...] / l_ref[...]
    for g in range(ATT_GROUP):
        o_ref[:, g * ATT_HD:(g + 1) * ATT_HD] = out[:, g * tq:(g + 1) * tq].T.astype(BF16)


def _attn(aq_t, ak, av_t):
    bsz, _, _, t = aq_t.shape
    tq = ATT_TQ
    nsub = t // ATT_TKS
    unroll = ATT_UNROLL if nsub % ATT_UNROLL == 0 else 1
    gw = ATT_GROUP * ATT_HD
    return pl.pallas_call(
        functools.partial(_attn_body, unroll=unroll),
        grid=(bsz, ATT_KV, t // tq),
        in_specs=[pl.BlockSpec((None, ATT_GROUP, ATT_HD, tq), lambda b, h, qi: (b, h, 0, qi)),
                  pl.BlockSpec((None, t, ATT_HD), lambda b, h, qi: (b, 0, h)),
                  pl.BlockSpec((None, None, ATT_HD, t), lambda b, h, qi: (b, h, 0, 0))],
        out_specs=pl.BlockSpec((None, tq, gw), lambda b, h, qi: (b, qi, h)),
        out_shape=jax.ShapeDtypeStruct((bsz, t, ATT_W), BF16),
        scratch_shapes=[pltpu.VMEM((ATT_HD, ATT_GROUP * tq), BF16),
                        pltpu.VMEM((ATT_TKS, ATT_GROUP * tq), F32),
                        pltpu.VMEM((1, ATT_GROUP * tq), F32),
                        pltpu.VMEM((1, ATT_GROUP * tq), F32),
                        pltpu.VMEM((ATT_HD, ATT_GROUP * tq), F32)],
        compiler_params=_cparams(("parallel", "parallel", "arbitrary")),
        name="attn",
    )(aq_t, ak, av_t)


def _memkv_body(m_ref, nw_ref, w_ref, o_ref):
    o_ref[...] = _dot(_rms(m_ref[...], nw_ref[...]).astype(BF16), w_ref[...]).astype(BF16)


def _memkv(mem, nw, w_ckv):
    bsz, m, d = mem.shape
    return pl.pallas_call(
        _memkv_body,
        grid=(bsz,),
        in_specs=[pl.BlockSpec((None, m, d), lambda b: (b, 0, 0)), _const_spec((1, d)),
                  _const_spec((d, 2 * X_W))],
        out_specs=pl.BlockSpec((None, m, 2 * X_W), lambda b: (b, 0, 0)),
        out_shape=jax.ShapeDtypeStruct((bsz, m, 2 * X_W), BF16),
        compiler_params=_cparams(("parallel",)),
        name="memkv",
    )(mem, nw, w_ckv)


def _post_body(x_ref, o2_ref, hg_ref, oa_ref, gh_ref, ga_ref, kv_ref,
               hgn_ref, wbh_ref, wba_ref, wo_ref, nx_ref, wcq_ref, wco_ref, nf_ref, wr_ref,
               h2_ref, u3_ref, aff_ref):
    o = o2_ref[0] + o2_ref[1]
    parts = []
    for h in range(HG_HEADS):
        oh = o[:, h * HG_D:(h + 1) * HG_D]
        parts.append(oh * lax.rsqrt(jnp.mean(oh * oh, axis=-1, keepdims=True) + EPS))
    hgv = hg_ref[...]
    oh = (jnp.concatenate(parts, axis=1) * hgn_ref[...] * (hgv * _sigmoid(hgv))).astype(BF16)
    merged = (_sigmoid(gh_ref[...]) * _dot(oh, wbh_ref[...])
              + _sigmoid(ga_ref[...]) * _dot(oa_ref[...], wba_ref[...]))
    h1 = x_ref[...] + _dot(merged.astype(BF16), wo_ref[...])

    u2 = _rms(h1, nx_ref[...]).astype(BF16)
    qx = (_dot(u2, wcq_ref[...]) * (X_HD ** -0.5)).astype(BF16)
    outs = []
    for h in range(X_HEADS):
        hs = slice(h * X_HD, (h + 1) * X_HD)
        s = _dot_nt(qx[:, hs], kv_ref[:, hs])
        p = jnp.exp(s - jnp.max(s, axis=1, keepdims=True))
        ov = _dot(p.astype(BF16), kv_ref[:, X_W + h * X_HD:X_W + (h + 1) * X_HD])
        outs.append(ov / jnp.sum(p, axis=1, keepdims=True))
    h2 = h1 + _dot(jnp.concatenate(outs, axis=1).astype(BF16), wco_ref[...])
    h2_ref[...] = h2

    u3 = _rms(h2, nf_ref[...]).astype(BF16)
    u3_ref[...] = u3
    logits = _dot_nt(wr_ref[...], u3)
    e = jnp.exp(logits - jnp.max(logits, axis=0, keepdims=True))
    aff_ref[...] = e / jnp.sum(e, axis=0, keepdims=True)


def _post(x, o2, hg, oa, gh, ga, kv, hgn, wbh, wba, wo, nx, wcq, wco, nf, wr_t):
    bsz, t, d = x.shape
    tm = TOKEN_TILE
    nt = t // tm
    m = kv.shape[1]
    tok = lambda w: pl.BlockSpec((None, tm, w), lambda b, i: (b, i, 0))
    return pl.pallas_call(
        _post_body,
        grid=(bsz, nt),
        in_specs=[tok(d), pl.BlockSpec((2, None, tm, HG_W), lambda b, i: (0, b, i, 0)), tok(HG_W),
                  tok(ATT_W), tok(d), tok(d), pl.BlockSpec((None, m, 2 * X_W), lambda b, i: (b, 0, 0)),
                  _const_spec((1, HG_W)), _const_spec((HG_W, d)), _const_spec((ATT_W, d)),
                  _const_spec((d, d)), _const_spec((1, d)), _const_spec((d, X_W)), _const_spec((X_W, d)),
                  _const_spec((1, d)), _const_spec((N_EXPERTS, d))],
        out_specs=[tok(d), tok(d), pl.BlockSpec((N_EXPERTS, tm), lambda b, i: (0, b * nt + i))],
        out_shape=[jax.ShapeDtypeStruct((bsz, t, d), F32), jax.ShapeDtypeStruct((bsz, t, d), BF16),
                   jax.ShapeDtypeStruct((N_EXPERTS, bsz * t), F32)],
        compiler_params=_cparams(("parallel", "parallel")),
        name="post",
    )(x, o2, hg, oa, gh, ga, kv, hgn, wbh, wba, wo, nx, wcq, wco, nf, wr_t)


def _route_body(aff_ref, pos_ref, off_ref, cnt_ref, *, cap):
    n = aff_ref.shape[1]
    nblk = n // ROUTE_BLOCK
    bits = pltpu.bitcast(aff_ref[...], I32)

    def count_ge(th):
        return jnp.sum(jnp.where(bits >= th, 1.0, 0.0), axis=1, keepdims=True)

    def bisect(_, lohi):
        lo, hi = lohi
        mid = lo + lax.shift_right_logical(hi - lo, 1)
        ok = count_ge(mid) >= cap
        return jnp.where(ok, mid, lo), jnp.where(ok, hi, mid)

    lo0 = jnp.zeros((N_EXPERTS, 1), I32)
    hi0 = jnp.full((N_EXPERTS, 1), 0x7F800000, I32)
    thr, _ = lax.fori_loop(0, 31, bisect, (lo0, hi0))
    n_gt = jnp.sum(jnp.where(bits > thr, 1.0, 0.0), axis=1, keepdims=True)
    need = cap - n_gt

    r_i = lax.broadcasted_iota(I32, (ROUTE_BLOCK, ROUTE_BLOCK), 0)
    c_i = lax.broadcasted_iota(I32, (ROUTE_BLOCK, ROUTE_BLOCK), 1)
    tri = jnp.where(r_i < c_i, 1.0, 0.0).astype(BF16)
    lane = lax.broadcasted_iota(I32, (N_EXPERTS, MAX_ROUTE_BLOCKS), 1)

    off_ref[...] = jnp.zeros_like(off_ref)
    cnt_ref[...] = jnp.zeros_like(cnt_ref)

    def scan(j, carry):
        run_sel, run_eq = carry
        cols = pl.ds(pl.multiple_of(j * ROUTE_BLOCK, ROUTE_BLOCK), ROUTE_BLOCK)
        b = pltpu.bitcast(aff_ref[:, cols], I32)
        eq = jnp.where(b == thr, 1.0, 0.0)
        eq_rank = run_eq + _dot(eq.astype(BF16), tri)
        sel = jnp.where(b > thr, 1.0, jnp.where(eq_rank < need, eq, 0.0))
        pos = run_sel + _dot(sel.astype(BF16), tri)
        pos_ref[:, cols] = jnp.where(sel > 0.0, pos, -1.0).astype(I32)
        c = jnp.sum(sel, axis=1, keepdims=True)
        off_ref[...] = jnp.where(lane == j, run_sel.astype(I32), off_ref[...])
        cnt_ref[...] = jnp.where(lane == j, c.astype(I32), cnt_ref[...])
        return run_sel + c, run_eq + jnp.sum(eq, axis=1, keepdims=True)

    z1 = n_gt * 0.0
    lax.fori_loop(0, nblk, scan, (z1, z1))


def _route(aff_t, cap):
    n = aff_t.shape[1]
    assert n % ROUTE_BLOCK == 0 and n // ROUTE_BLOCK <= MAX_ROUTE_BLOCKS
    tbl = jax.ShapeDtypeStruct((N_EXPERTS, MAX_ROUTE_BLOCKS), I32)
    return pl.pallas_call(
        functools.partial(_route_body, cap=cap),
        out_shape=[jax.ShapeDtypeStruct((N_EXPERTS, n), I32), tbl, tbl],
        compiler_params=pltpu.CompilerParams(vmem_limit_bytes=V7X_VMEM_LIMIT_BYTES),
        name="route",
    )(aff_t)


_STAGE_SLOTS = 8


def _dispatch_body(off_s, cnt_s, pos_ref, u_ref, xe_ref, stage_ref, tail_ref, sem_ref, ctr_ref, *, cap_pad):
    j = pl.program_id(0)
    last = pl.num_programs(0) - 1

    @pl.when(j == 0)
    def _():
        tail_ref[...] = jnp.zeros_like(tail_ref)
        ctr_ref[0] = 0
        ctr_ref[1] = 0

    def copy(slot, row):
        return pltpu.make_async_copy(stage_ref.at[slot], xe_ref.at[pl.ds(row, ROW_CHUNK), :], sem_ref.at[slot])

    def wait_oldest():
        copy(ctr_ref[1] % _STAGE_SLOTS, 0).wait()
        ctr_ref[1] = ctr_ref[1] + 1

    def drain():
        lax.fori_loop(ctr_ref[1], ctr_ref[0], lambda _, c: (wait_oldest(), c)[1], 0)

    drain()
    u = u_ref[...]
    r_i = lax.broadcasted_iota(I32, (ROW_CHUNK, ROUTE_BLOCK), 0)

    def expert(e, carry):
        off = off_s[e * MAX_ROUTE_BLOCKS + j]
        c = cnt_s[e * MAX_ROUTE_BLOCKS + j]
        a = off & (ROW_ALIGN - 1)
        base = off - a
        nch = jnp.where(c > 0, (a + c + ROW_CHUNK - 1) // ROW_CHUNK, 0)
        pos = pos_ref[pl.ds(e, 1), :]

        def chunk(ch, carry):
            onehot = jnp.where(r_i == pos - (base + ch * ROW_CHUNK), 1.0, 0.0).astype(BF16)
            rows = _dot(onehot, u)
            carried = jnp.where(ch == 0, tail_ref[e].astype(F32), 0.0)
            rows = jnp.concatenate([rows[:ROW_ALIGN] + carried, rows[ROW_ALIGN:]], axis=0)

            @pl.when(ctr_ref[0] - ctr_ref[1] >= _STAGE_SLOTS)
            def _():
                wait_oldest()

            slot = ctr_ref[0] % _STAGE_SLOTS
            stage_ref[slot] = rows.astype(BF16)
            row0 = pl.multiple_of(e * cap_pad + base + ch * ROW_CHUNK, ROW_ALIGN)
            copy(slot, row0).start()
            ctr_ref[0] = ctr_ref[0] + 1
            return carry

        lax.fori_loop(0, nch, chunk, 0)

        @pl.when(c > 0)
        def _():
            end = a + c
            part = end & (ROW_ALIGN - 1)
            slot = (ctr_ref[0] - 1) % _STAGE_SLOTS
            r0 = pl.multiple_of((end - part) & (ROW_CHUNK - 1), ROW_ALIGN)
            keep = stage_ref[slot, pl.ds(r0, ROW_ALIGN), :]
            tail_ref[e] = jnp.where(part > 0, keep, jnp.zeros_like(keep))

        return carry

    lax.fori_loop(0, N_EXPERTS, expert, 0)

    @pl.when(j == last)
    def _():
        drain()
        stage_ref[0] = jnp.zeros((ROW_CHUNK, stage_ref.shape[2]), BF16)
        pads = [copy(0, e * cap_pad + cap_pad - ROW_CHUNK) for e in range(N_EXPERTS)]
        for cp in pads:
            cp.start()
        for cp in pads:
            cp.wait()


def _dispatch(off, cnt, pos_t, u, cap):
    n, d = u.shape
    cap_pad = cap + ROW_CHUNK
    nblk = n // ROUTE_BLOCK
    return pl.pallas_call(
        functools.partial(_dispatch_body, cap_pad=cap_pad),
        grid_spec=pltpu.PrefetchScalarGridSpec(
            num_scalar_prefetch=2,
            grid=(nblk,),
            in_specs=[pl.BlockSpec((N_EXPERTS, ROUTE_BLOCK), lambda j, *_: (0, j)),
                      pl.BlockSpec((ROUTE_BLOCK, d), lambda j, *_: (j, 0))],
            out_specs=pl.BlockSpec(memory_space=pl.ANY),
            scratch_shapes=[pltpu.VMEM((_STAGE_SLOTS, ROW_CHUNK, d), BF16),
                            pltpu.VMEM((N_EXPERTS, ROW_ALIGN, d), BF16),
                            pltpu.SemaphoreType.DMA((_STAGE_SLOTS,)),
                            pltpu.SMEM((2,), I32)]),
        out_shape=jax.ShapeDtypeStruct((N_EXPERTS * cap_pad, d), BF16),
        compiler_params=_cparams(("arbitrary",)),
        name="dispatch",
    )(off, cnt, pos_t, u)


_FFN_COLS = 512


def _ffn_body(x_ref, wg_ref, wu_ref, wd_ref, o_ref):
    x = x_ref[...]
    acc = None
    for c in range(D_EXPERT // _FFN_COLS):
        cs = slice(c * _FFN_COLS, (c + 1) * _FFN_COLS)
        g = _dot(x, wg_ref[:, cs])
        hid = (g * _sigmoid(g) * _dot(x, wu_ref[:, cs])).astype(BF16)
        part = _dot(hid, wd_ref[cs, :])
        acc = part if acc is None else acc + part
    o_ref[...] = acc.astype(BF16)


def _ffn(xe, wg, wu, wd, cap):
    e, _, d = xe.shape
    tm = 512 if cap % 512 == 0 else ROW_CHUNK
    wspec = lambda a, b: pl.BlockSpec((None, a, b), lambda ei, i: (ei, 0, 0))
    return pl.pallas_call(
        _ffn_body,
        grid=(e, cap // tm),
        in_specs=[pl.BlockSpec((None, tm, d), lambda ei, i: (ei, i, 0)),
                  wspec(d, D_EXPERT), wspec(d, D_EXPERT), wspec(D_EXPERT, d)],
        out_specs=pl.BlockSpec((None, tm, d), lambda ei, i: (ei, i, 0)),
        out_shape=jax.ShapeDtypeStruct((e, cap, d), BF16),
        compiler_params=_cparams(("parallel", "parallel")),
        name="ffn",
    )(xe, wg, wu, wd)


_GATE_TERMS = 2


def _combine_body(off_s, cnt_s, pos_ref, aff_ref, h_ref, nw_ref, ye_ref, y_ref, rows_ref, lhs_ref, xbuf_ref,
                  sem_ref, xsem_ref, acc_ref, *, cap):
    j = pl.program_id(0)
    nblk = pl.num_programs(0)
    per_e = _GATE_TERMS * ROW_CHUNK

    def geom(e, jj):
        off = off_s[e * MAX_ROUTE_BLOCKS + jj]
        c = cnt_s[e * MAX_ROUTE_BLOCKS + jj]
        a = off & (ROW_ALIGN - 1)
        return off - a, jnp.where(c > 0, (a + c + ROW_CHUNK - 1) // ROW_CHUNK, 0)

    def window(base, ch):
        lo = base + ch * ROW_CHUNK
        return lo, pl.multiple_of(jnp.minimum(lo, cap - ROW_CHUNK), ROW_ALIGN)

    def src(e, start):
        return ye_ref.at[pl.ds(pl.multiple_of(e * cap + start, ROW_ALIGN), ROW_CHUNK), :]

    def first_chunks(jj, slot):
        cps = []
        for e in range(N_EXPERTS):
            base, _ = geom(e, jj)
            _, start = window(base, 0)
            for k in range(_GATE_TERMS):
                cps.append(pltpu.make_async_copy(
                    src(e, start), rows_ref.at[slot, pl.ds(e * per_e + k * ROW_CHUNK, ROW_CHUNK), :],
                    sem_ref.at[slot]))
        return cps

    @pl.when(j == 0)
    def _():
        for cp in first_chunks(0, 0):
            cp.start()

    @pl.when(j + 1 < nblk)
    def _():
        for cp in first_chunks(j + 1, (j + 1) % 2):
            cp.start()

    def gate_terms(gate):
        g_hi = gate.astype(BF16).astype(F32)
        return g_hi, gate - g_hi

    pos_c = pos_ref[...].T
    aff_c = aff_ref[...].T
    lane = lax.broadcasted_iota(I32, (ROUTE_BLOCK, ROW_CHUNK), 1)
    for e in range(N_EXPERTS):
        base, _ = geom(e, j)
        lo, start = window(base, 0)
        r = pos_c[:, e:e + 1] - start
        hit = lane == jnp.where(r >= lo - start, r, -1)
        for k, g in enumerate(gate_terms(aff_c[:, e:e + 1])):
            lhs_ref[:, e * per_e + k * ROW_CHUNK:e * per_e + (k + 1) * ROW_CHUNK] = (
                jnp.where(hit, g, 0.0).astype(BF16))

    slot = j % 2
    for cp in first_chunks(j, slot):
        cp.wait()
    acc_ref[...] = _dot(lhs_ref[...], rows_ref[slot])

    r_i = lax.broadcasted_iota(I32, (ROW_CHUNK, ROUTE_BLOCK), 0)

    def expert(e, carry):
        base, nch = geom(e, j)

        def extra(ch, carry):
            lo, start = window(base, ch)
            cp = pltpu.make_async_copy(src(e, start), xbuf_ref, xsem_ref)
            cp.start()
            cp.wait()
            pos = pos_ref[pl.ds(e, 1), :]
            hit = (r_i == pos - start) & (pos >= lo)
            for g in gate_terms(aff_ref[pl.ds(e, 1), :]):
                acc_ref[...] += _dot_tn(jnp.where(hit, g, 0.0).astype(BF16), xbuf_ref[...])
            return carry

        return lax.fori_loop(1, jnp.maximum(nch, 1), extra, carry)

    lax.fori_loop(0, N_EXPERTS, expert, 0)
    y_ref[...] = _rms(h_ref[...] + acc_ref[...], nw_ref[...])


def _combine(off, cnt, pos_t, aff_t, h2, nw, ye, cap):
    n, d = h2.shape
    nblk = n // ROUTE_BLOCK
    eblk = pl.BlockSpec((N_EXPERTS, ROUTE_BLOCK), lambda j, *_: (0, j))
    tok = pl.BlockSpec((ROUTE_BLOCK, d), lambda j, *_: (j, 0))
    kdim = N_EXPERTS * _GATE_TERMS * ROW_CHUNK
    return pl.pallas_call(
        functools.partial(_combine_body, cap=cap),
        grid_spec=pltpu.PrefetchScalarGridSpec(
            num_scalar_prefetch=2,
            grid=(nblk,),
            in_specs=[eblk, eblk, tok, pl.BlockSpec((1, d), lambda j, *_: (0, 0)),
                      pl.BlockSpec(memory_space=pl.ANY)],
            out_specs=tok,
            scratch_shapes=[pltpu.VMEM((2, kdim, d), BF16),
                            pltpu.VMEM((ROUTE_BLOCK, kdim), BF16),
                            pltpu.VMEM((ROW_CHUNK, d), BF16),
                            pltpu.SemaphoreType.DMA((2,)),
                            pltpu.SemaphoreType.DMA(()),
                            pltpu.VMEM((ROUTE_BLOCK, d), F32)]),
        out_shape=jax.ShapeDtypeStruct((n, d), F32),
        compiler_params=_cparams(("arbitrary",)),
        name="combine",
    )(off, cnt, pos_t, aff_t, h2, nw, ye)


def _trunk(x, mem, lb, p):
    bsz, t, d = x.shape
    n = bsz * t
    cap = max(1, EC_FACTOR * n // N_EXPERTS)
    assert cap % ROW_CHUNK == 0 and t % HG_BLOCK == 0 and t % ATT_TKS == 0
    cos, s1, s2 = _rope_tables(t)
    hq, zf, hi, hg, aq, ak, av, gh, ga = _proj(x, p["norm_mix"], p["w_in"], cos, s1, s2, p["q_norm"], p["k_norm"])
    o2 = _hgrn(hq, zf, hi, lb)
    oa = _attn(aq, ak, av)
    kv = _memkv(mem, p["norm_mem"], p["w_ckv"])
    h2, u3, aff_t = _post(x, o2, hg, oa, gh, ga, kv, p["hgrn_norm"], p["w_br_hgrn"], p["w_br_attn"], p["w_out"],
                          p["norm_x"], p["w_cq"], p["w_co"], p["norm_ffn"], p["w_router_t"])
    pos_t, off, cnt = _route(aff_t, cap)
    off, cnt = off.reshape(-1), cnt.reshape(-1)
    xe = _dispatch(off, cnt, pos_t, u3.reshape(n, d), cap)
    ye = _ffn(xe.reshape(N_EXPERTS, cap + ROW_CHUNK, d), p["w_gate"], p["w_up"], p["w_down"], cap)
    y = _combine(off, cnt, pos_t, aff_t, h2.reshape(n, d), p["norm_final"], ye.reshape(N_EXPERTS * cap, d), cap)
    return y.reshape(bsz, t, d)


def kernel(x_prompt, x_sample, mem_prompt, mem_sample, norm_mix, w_in, lb_logits, hgrn_norm, q_norm, k_norm,
           w_br_hgrn, w_br_attn, w_out, norm_x, norm_mem, w_cq, w_ckv, w_co, norm_ffn, w_router, w_gate, w_up,
           w_down, norm_final):
    row = lambda v: v.reshape(1, -1).astype(F32)
    p = {
        "norm_mix": row(norm_mix[0]), "w_in": w_in[0].astype(BF16),
        "hgrn_norm": row(hgrn_norm[0]), "q_norm": row(q_norm[0]), "k_norm": row(k_norm[0]),
        "w_br_hgrn": w_br_hgrn[0].astype(BF16), "w_br_attn": w_br_attn[0].astype(BF16),
        "w_out": w_out[0].astype(BF16), "norm_x": row(norm_x[0]), "norm_mem": row(norm_mem[0]),
        "w_cq": w_cq[0].astype(BF16), "w_ckv": w_ckv[0].astype(BF16), "w_co": w_co[0].astype(BF16),
        "norm_ffn": row(norm_ffn[0]), "w_router_t": w_router[0].T.astype(BF16),
        "w_gate": w_gate[0].astype(BF16), "w_up": w_up[0].astype(BF16), "w_down": w_down[0].astype(BF16),
        "norm_final": row(norm_final),
    }
    lb = jnp.cumsum(jax.nn.softmax(lb_logits.astype(F32), axis=1), axis=1)[:, 0]
    return (_trunk(x_prompt, mem_prompt, lb, p), _trunk(x_sample, mem_sample, lb, p))
```

```python
import functools

import jax
import jax.numpy as jnp
from jax import lax
from jax.experimental import pallas as pl
from jax.experimental.pallas import tpu as pltpu

F32 = jnp.float32
BF16 = jnp.bfloat16
I32 = jnp.int32

D_MODEL = 1024
EPS = 1e-6
HG_HEADS = 4
HG_D = 128
HG_W = HG_HEADS * HG_D
HG_CHUNK = 64
HG_LEAF = 16
HG_BLOCK = 512
ATT_HEADS = 8
ATT_KV = 2
ATT_GROUP = ATT_HEADS // ATT_KV
ATT_HD = 128
ATT_W = ATT_HEADS * ATT_HD
ATT_KV_W = ATT_KV * ATT_HD
GRID_W = 64
ROPE_HALF = ATT_HD // 4
ROPE_THETA = 10000.0
ATT_TQ = 256
ATT_TKS = 512
ATT_UNROLL = 4
ATT_MAX_BOUNDED_SCORE = 60.0
X_HEADS = 4
X_HD = 128
X_W = X_HEADS * X_HD
N_EXPERTS = 16
EC_FACTOR = 2
D_EXPERT = 2048
ROUTE_BLOCK = 512
ROW_CHUNK = 128
ROW_ALIGN = 16
MAX_ROUTE_BLOCKS = 128
_IN_SIZES = (HG_W, HG_W, HG_W, HG_W, HG_W, ATT_W, ATT_KV_W, ATT_KV_W, D_MODEL, D_MODEL)
_IN_OFF = tuple(sum(_IN_SIZES[:i]) for i in range(len(_IN_SIZES) + 1))
IN_WIDTH = _IN_OFF[-1]

TOKEN_TILE = 256
V7X_VMEM_LIMIT_BYTES = 56 * 1024 * 1024
LOG2E = 1.4426950408889634


def _cparams(sem):
    return pltpu.CompilerParams(dimension_semantics=sem, vmem_limit_bytes=V7X_VMEM_LIMIT_BYTES)


def _const_spec(shape):
    nd = len(shape)
    return pl.BlockSpec(shape, lambda *_: (0,) * nd, pipeline_mode=pl.Buffered(1))


def _rms(x, g):
    return x * lax.rsqrt(jnp.mean(x * x, axis=-1, keepdims=True) + EPS) * g


def _sigmoid(x):
    return 1.0 / (1.0 + jnp.exp(-x))


def _dot(a, b):
    return jnp.dot(a, b, preferred_element_type=F32)


def _dot_nt(a, b):
    return lax.dot_general(a, b, (((1,), (1,)), ((), ())), preferred_element_type=F32)


def _dot_tn(a, b):
    return lax.dot_general(a, b, (((0,), (0,)), ((), ())), preferred_element_type=F32)


def _proj_body(x_ref, nw_ref, w_ref, cos_ref, s1_ref, s2_ref, qg_ref, kg_ref,
               hq_ref, zf_ref, hi_ref, hg_ref, aq_ref, ak_ref, av_ref, gh_ref, ga_ref):
    u = _rms(x_ref[...], nw_ref[...]).astype(BF16)

    def mm(i):
        return _dot(u, w_ref[:, _IN_OFF[i]:_IN_OFF[i + 1]])

    hq_ref[...] = mm(0).astype(BF16)
    zf_ref[0] = mm(1)
    zf_ref[1] = mm(2)
    hi_ref[...] = mm(3).astype(BF16)
    hg_ref[...] = mm(4)
    c, s1, s2 = cos_ref[...], s1_ref[...], s2_ref[...]

    def norm_rope(z, g, scale):
        y = _rms(z, g)
        y = y * c + pltpu.roll(y, ATT_HD - ROPE_HALF, 1) * s1 + pltpu.roll(y, ROPE_HALF, 1) * s2
        return y * scale

    zq = mm(5)
    qscale = (ATT_HD ** -0.5) * LOG2E
    for h in range(ATT_HEADS):
        hs = slice(h * ATT_HD, (h + 1) * ATT_HD)
        aq_ref[h] = norm_rope(zq[:, hs], qg_ref[...], qscale).T.astype(BF16)
    zk = mm(6)
    for h in range(ATT_KV):
        hs = slice(h * ATT_HD, (h + 1) * ATT_HD)
        ak_ref[:, hs] = norm_rope(zk[:, hs], kg_ref[...], 1.0).astype(BF16)
    zv = mm(7)
    for h in range(ATT_KV):
        av_ref[h] = zv[:, h * ATT_HD:(h + 1) * ATT_HD].T.astype(BF16)
    gh_ref[...] = mm(8)
    ga_ref[...] = mm(9)


def _proj(x, nw, w_in, cos, s1, s2, qg, kg):
    bsz, t, d = x.shape
    tm = TOKEN_TILE
    tok = lambda w: pl.BlockSpec((None, tm, w), lambda b, i: (b, i, 0))
    tab = pl.BlockSpec((tm, ATT_HD), lambda b, i: (i, 0))
    sds = lambda w, dt: jax.ShapeDtypeStruct((bsz, t, w), dt)
    head_t = lambda nh: pl.BlockSpec((None, nh, ATT_HD, tm), lambda b, i: (b, 0, 0, i))
    sds_t = lambda nh: jax.ShapeDtypeStruct((bsz, nh, ATT_HD, t), BF16)
    return pl.pallas_call(
        _proj_body,
        grid=(bsz, t // tm),
        in_specs=[tok(d), _const_spec((1, d)), _const_spec((d, IN_WIDTH)), tab, tab, tab,
                  _const_spec((1, ATT_HD)), _const_spec((1, ATT_HD))],
        out_specs=[tok(HG_W),
                   pl.BlockSpec((2, None, tm, HG_W), lambda b, i: (0, b, i, 0)),
                   tok(HG_W), tok(HG_W), head_t(ATT_HEADS), tok(ATT_KV_W), head_t(ATT_KV),
                   tok(D_MODEL), tok(D_MODEL)],
        out_shape=[sds(HG_W, BF16), jax.ShapeDtypeStruct((2, bsz, t, HG_W), F32),
                   sds(HG_W, BF16), sds(HG_W, F32), sds_t(ATT_HEADS), sds(ATT_KV_W, BF16),
                   sds_t(ATT_KV), sds(D_MODEL, F32), sds(D_MODEL, F32)],
        compiler_params=_cparams(("parallel", "parallel")),
        name="proj",
    )(x, nw, w_in, cos, s1, s2, qg, kg)


def _rope_tables(t):
    pos_t = jnp.arange(t, dtype=I32)
    pos = jnp.stack([pos_t // GRID_W, pos_t % GRID_W], axis=-1).astype(F32)
    inv = jnp.power(ROPE_THETA, -jnp.arange(0, 2 * ROPE_HALF, 2, dtype=F32) / (2 * ROPE_HALF))
    ang = pos[:, :, None] * inv
    cos, sin = jnp.cos(ang), jnp.sin(ang)
    zero = jnp.zeros_like(sin)
    c = jnp.concatenate([cos, cos], axis=-1).reshape(t, ATT_HD)
    s1 = jnp.concatenate([-sin, zero], axis=-1).reshape(t, ATT_HD)
    s2 = jnp.concatenate([zero, sin], axis=-1).reshape(t, ATT_HD)
    return c, s1, s2


def _split3(x):
    a = x.astype(BF16)
    r = x - a.astype(F32)
    b = r.astype(BF16)
    c = (r - b.astype(F32)).astype(BF16)
    return a, b, c


def _hgrn_chunk(q, z, v, lb, st, rev):
    C, LS = HG_CHUNK, HG_LEAF
    nb = C // LS
    f = lb + (1.0 - lb) * _sigmoid(z)
    g = jnp.log(f)
    kk = 1.0 - f
    qf = q.astype(F32)
    r_i = lax.broadcasted_iota(I32, (C, C), 0)
    c_i = lax.broadcasted_iota(I32, (C, C), 1)
    before = (c_i >= r_i) if rev else (c_i <= r_i)
    tri = jnp.where(before, 1.0, 0.0).astype(BF16)
    g1, g2, g3 = _split3(g)
    b2 = (_dot(tri, g1) + _dot(tri, g2) + _dot(tri, g3)) * LOG2E
    btot = b2[0:1] if rev else b2[C - 1:C]

    qs = (qf * jnp.exp2(b2)).astype(BF16)
    o = _dot_nt(qs, st.astype(BF16))
    kd = (kk * jnp.exp2(btot - b2)).astype(BF16)
    st_new = st * jnp.exp2(btot) + _dot_tn(v, kd)

    cols = range(1, nb) if rev else range(nb - 1)
    lhs, rhs_rows = [], {}
    zeros_l = jnp.zeros((LS, HG_D), BF16)
    for p, j in enumerate(cols):
        rr = LS * j if rev else LS * (j + 1) - 1
        ref = b2[rr:rr + 1]
        rows = slice(0, LS * j) if rev else slice(LS * (j + 1), C)
        scaled = (qf[rows] * jnp.exp2(b2[rows] - ref)).astype(BF16)
        pad = jnp.zeros((C - scaled.shape[0], HG_D), BF16)
        lhs.append(jnp.concatenate([scaled, pad] if rev else [pad, scaled], axis=0))
        kj = (kk[LS * j:LS * (j + 1)] * jnp.exp2(ref - b2[LS * j:LS * (j + 1)])).astype(BF16)
        rhs_rows[j] = jnp.concatenate([kj if i == p else zeros_l for i in range(nb - 1)], axis=1)
    zero_row = jnp.zeros((LS, HG_D * (nb - 1)), BF16)
    kmat = jnp.concatenate([rhs_rows.get(j, zero_row) for j in range(nb)], axis=0)
    a_off = _dot_nt(jnp.concatenate(lhs, axis=1), kmat)

    SUB = 8
    lane = lax.broadcasted_iota(I32, (SUB, C), 1)
    tiles = [jnp.zeros((SUB, C), F32) for _ in range(C // SUB)]
    for s in range(C):
        blk = s // LS
        first, last = (LS * blk // SUB, s // SUB) if rev else (s // SUB, LS * (blk + 1) // SUB - 1)
        brow, krow = b2[s:s + 1], kk[s:s + 1]
        for t in range(first, last + 1):
            rs = slice(SUB * t, SUB * (t + 1))
            d = b2[rs] - brow
            if t == s // SUB:
                d = jnp.minimum(d, 0.0)
            col = jnp.sum(qf[rs] * jnp.exp2(d) * krow, axis=1, keepdims=True)
            tiles[t] = jnp.where(lane == s, col, tiles[t])
    a_diag = jnp.where((r_i <= c_i) if rev else (r_i >= c_i), jnp.concatenate(tiles, axis=0), 0.0)
    o = o + _dot((a_off + a_diag).astype(BF16), v)
    return o, st_new


def _hgrn_body(q_ref, z_ref, v_ref, lb_ref, o_ref, st_ref):
    d = pl.program_id(1)
    j = pl.program_id(2)
    nch = HG_BLOCK // HG_CHUNK

    @pl.when(j == 0)
    def _():
        st_ref[...] = jnp.zeros_like(st_ref)

    def run(rev):
        def step(i, carry):
            c = (nch - 1 - i) if rev else i
            rows = pl.ds(pl.multiple_of(c * HG_CHUNK, HG_CHUNK), HG_CHUNK)
            for h in range(HG_HEADS):
                hs = slice(h * HG_D, (h + 1) * HG_D)
                o, st = _hgrn_chunk(q_ref[rows, hs], z_ref[rows, hs], v_ref[rows, hs],
                                    lb_ref[int(rev):int(rev) + 1, hs], st_ref[h], rev)
                o_ref[rows, hs] = o
                st_ref[h] = st
            return carry
        lax.fori_loop(0, nch, step, 0)

    @pl.when(d == 0)
    def _():
        run(False)

    @pl.when(d == 1)
    def _():
        run(True)


def _hgrn(hq, zf, hi, lb):
    bsz, t, _ = hq.shape
    nblk = t // HG_BLOCK

    def blk(d, j):
        return jnp.where(d == 0, j, nblk - 1 - j)

    tok = pl.BlockSpec((None, HG_BLOCK, HG_W), lambda b, d, j: (b, blk(d, j), 0))
    dirtok = pl.BlockSpec((None, None, HG_BLOCK, HG_W), lambda b, d, j: (d, b, blk(d, j), 0))
    return pl.pallas_call(
        _hgrn_body,
        grid=(bsz, 2, nblk),
        in_specs=[tok, dirtok, tok, _const_spec((2, HG_W))],
        out_specs=dirtok,
        out_shape=jax.ShapeDtypeStruct((2, bsz, t, HG_W), F32),
        scratch_shapes=[pltpu.VMEM((HG_HEADS, HG_D, HG_D), F32)],
        compiler_params=_cparams(("parallel", "arbitrary", "arbitrary")),
        name="hgrn",
    )(hq, zf, hi, lb)


def _attn_body(bound_s, q_ref, k_ref, v_ref, o_ref, qt_ref, s0_ref, m_ref, l_ref, acc_ref, *, unroll, bounded):
    tq, tks = ATT_TQ, ATT_TKS
    nsub = k_ref.shape[0] // tks
    for g in range(ATT_GROUP):
        qt_ref[:, g * tq:(g + 1) * tq] = q_ref[g]
    m_ref[...] = jnp.full_like(m_ref, -jnp.inf)
    l_ref[...] = jnp.zeros_like(l_ref)
    acc_ref[...] = jnp.zeros_like(acc_ref)
    qt = qt_ref[...]
    s0_ref[...] = _dot(k_ref[0:tks, :], qt)
    bound = bound_s[0]

    def group(i, carry):
        m, l, acc = m_ref[...], l_ref[...], acc_ref[...]
        s_next = s0_ref[...]
        for u in range(unroll):
            c = i * unroll + u
            s = s_next
            nxt = jnp.minimum(c + 1, nsub - 1)
            s_next = _dot(k_ref[pl.ds(pl.multiple_of(nxt * tks, tks), tks), :], qt)
            vt = v_ref[:, pl.ds(pl.multiple_of(c * tks, tks), tks)]
            if bounded:
                p = jnp.exp2(s - bound)
                l = l + jnp.sum(p, axis=0, keepdims=True)
                acc = acc + _dot(vt, p.astype(BF16))
            else:
                m_new = jnp.maximum(m, jnp.max(s, axis=0, keepdims=True))
                alpha = jnp.exp2(m - m_new)
                p = jnp.exp2(s - m_new)
                l = alpha * l + jnp.sum(p, axis=0, keepdims=True)
                acc = alpha * acc + _dot(vt, p.astype(BF16))
                m = m_new
        s0_ref[...] = s_next
        m_ref[...], l_ref[...], acc_ref[...] = m, l, acc
        return carry

    lax.fori_loop(0, nsub // unroll, group, 0)
    out = acc_ref[...] / l_ref[...]
    for g in range(ATT_GROUP):
        o_ref[:, g * ATT_HD:(g + 1) * ATT_HD] = out[:, g * tq:(g + 1) * tq].T.astype(BF16)


def _attn_call(bound, aq_t, ak, av_t, bounded):
    bsz, _, _, t = aq_t.shape
    tq = ATT_TQ
    nsub = t // ATT_TKS
    unroll = ATT_UNROLL if nsub % ATT_UNROLL == 0 else 1
    gw = ATT_GROUP * ATT_HD
    return pl.pallas_call(
        functools.partial(_attn_body, unroll=unroll, bounded=bounded),
        grid_spec=pltpu.PrefetchScalarGridSpec(
            num_scalar_prefetch=1,
            grid=(bsz, ATT_KV, t // tq),
            in_specs=[pl.BlockSpec((None, ATT_GROUP, ATT_HD, tq), lambda b, h, qi, *_: (b, h, 0, qi)),
                      pl.BlockSpec((None, t, ATT_HD), lambda b, h, qi, *_: (b, 0, h)),
                      pl.BlockSpec((None, None, ATT_HD, t), lambda b, h, qi, *_: (b, h, 0, 0))],
            out_specs=pl.BlockSpec((None, tq, gw), lambda b, h, qi, *_: (b, qi, h)),
            scratch_shapes=[pltpu.VMEM((ATT_HD, ATT_GROUP * tq), BF16),
                            pltpu.VMEM((ATT_TKS, ATT_GROUP * tq), F32),
                            pltpu.VMEM((1, ATT_GROUP * tq), F32),
                            pltpu.VMEM((1, ATT_GROUP * tq), F32),
                            pltpu.VMEM((ATT_HD, ATT_GROUP * tq), F32)]),
        out_shape=jax.ShapeDtypeStruct((bsz, t, ATT_W), BF16),
        compiler_params=_cparams(("parallel", "parallel", "arbitrary")),
        name="attn_bounded" if bounded else "attn",
    )(bound, aq_t, ak, av_t)


def _attn(bound, aq_t, ak, av_t):
    return lax.cond(bound[0] <= ATT_MAX_BOUNDED_SCORE,
                    functools.partial(_attn_call, bounded=True),
                    functools.partial(_attn_call, bounded=False),
                    bound, aq_t, ak, av_t)


def _score_bound(q_gain, k_gain):
    qscale = (ATT_HD ** -0.5) * LOG2E
    b = 1.01 * ATT_HD * qscale * jnp.max(jnp.abs(q_gain)) * jnp.max(jnp.abs(k_gain))
    return b.reshape(1).astype(F32)


def _memkv_body(m_ref, nw_ref, w_ref, o_ref):
    o_ref[...] = _dot(_rms(m_ref[...], nw_ref[...]).astype(BF16), w_ref[...]).astype(BF16)


def _memkv(mem, nw, w_ckv):
    bsz, m, d = mem.shape
    return pl.pallas_call(
        _memkv_body,
        grid=(bsz,),
        in_specs=[pl.BlockSpec((None, m, d), lambda b: (b, 0, 0)), _const_spec((1, d)),
                  _const_spec((d, 2 * X_W))],
        out_specs=pl.BlockSpec((None, m, 2 * X_W), lambda b: (b, 0, 0)),
        out_shape=jax.ShapeDtypeStruct((bsz, m, 2 * X_W), BF16),
        compiler_params=_cparams(("parallel",)),
        name="memkv",
    )(mem, nw, w_ckv)


def _post_body(x_ref, o2_ref, hg_ref, oa_ref, gh_ref, ga_ref, kv_ref,
               hgn_ref, wbh_ref, wba_ref, wo_ref, nx_ref, wcq_ref, wco_ref, nf_ref, wr_ref,
               h2_ref, u3_ref, aff_ref):
    o = o2_ref[0] + o2_ref[1]
    parts = []
    for h in range(HG_HEADS):
        oh = o[:, h * HG_D:(h + 1) * HG_D]
        parts.append(oh * lax.rsqrt(jnp.mean(oh * oh, axis=-1, keepdims=True) + EPS))
    hgv = hg_ref[...]
    oh = (jnp.concatenate(parts, axis=1) * hgn_ref[...] * (hgv * _sigmoid(hgv))).astype(BF16)
    merged = (_sigmoid(gh_ref[...]) * _dot(oh, wbh_ref[...])
              + _sigmoid(ga_ref[...]) * _dot(oa_ref[...], wba_ref[...]))
    h1 = x_ref[...] + _dot(merged.astype(BF16), wo_ref[...])

    u2 = _rms(h1, nx_ref[...]).astype(BF16)
    qx = (_dot(u2, wcq_ref[...]) * (X_HD ** -0.5)).astype(BF16)
    outs = []
    for h in range(X_HEADS):
        hs = slice(h * X_HD, (h + 1) * X_HD)
        s = _dot_nt(qx[:, hs], kv_ref[:, hs])
        p = jnp.exp(s - jnp.max(s, axis=1, keepdims=True))
        ov = _dot(p.astype(BF16), kv_ref[:, X_W + h * X_HD:X_W + (h + 1) * X_HD])
        outs.append(ov / jnp.sum(p, axis=1, keepdims=True))
    h2 = h1 + _dot(jnp.concatenate(outs, axis=1).astype(BF16), wco_ref[...])
    h2_ref[...] = h2

    u3 = _rms(h2, nf_ref[...]).astype(BF16)
    u3_ref[...] = u3
    logits = _dot_nt(wr_ref[...], u3)
    e = jnp.exp(logits - jnp.max(logits, axis=0, keepdims=True))
    aff_ref[...] = e / jnp.sum(e, axis=0, keepdims=True)


def _post(x, o2, hg, oa, gh, ga, kv, hgn, wbh, wba, wo, nx, wcq, wco, nf, wr_t):
    bsz, t, d = x.shape
    tm = TOKEN_TILE
    nt = t // tm
    m = kv.shape[1]
    tok = lambda w: pl.BlockSpec((None, tm, w), lambda b, i: (b, i, 0))
    return pl.pallas_call(
        _post_body,
        grid=(bsz, nt),
        in_specs=[tok(d), pl.BlockSpec((2, None, tm, HG_W), lambda b, i: (0, b, i, 0)), tok(HG_W),
                  tok(ATT_W), tok(d), tok(d), pl.BlockSpec((None, m, 2 * X_W), lambda b, i: (b, 0, 0)),
                  _const_spec((1, HG_W)), _const_spec((HG_W, d)), _const_spec((ATT_W, d)),
                  _const_spec((d, d)), _const_spec((1, d)), _const_spec((d, X_W)), _const_spec((X_W, d)),
                  _const_spec((1, d)), _const_spec((N_EXPERTS, d))],
        out_specs=[tok(d), tok(d), pl.BlockSpec((N_EXPERTS, tm), lambda b, i: (0, b * nt + i))],
        out_shape=[jax.ShapeDtypeStruct((bsz, t, d), F32), jax.ShapeDtypeStruct((bsz, t, d), BF16),
                   jax.ShapeDtypeStruct((N_EXPERTS, bsz * t), F32)],
        compiler_params=_cparams(("parallel", "parallel")),
        name="post",
    )(x, o2, hg, oa, gh, ga, kv, hgn, wbh, wba, wo, nx, wcq, wco, nf, wr_t)


def _route_body(aff_ref, pos_ref, off_ref, cnt_ref, *, cap):
    n = aff_ref.shape[1]
    nblk = n // ROUTE_BLOCK
    bits = pltpu.bitcast(aff_ref[...], I32)

    def count_ge(th):
        return jnp.sum(jnp.where(bits >= th, 1.0, 0.0), axis=1, keepdims=True)

    def bisect(_, lohi):
        lo, hi = lohi
        mid = lo + lax.shift_right_logical(hi - lo, 1)
        ok = count_ge(mid) >= cap
        return jnp.where(ok, mid, lo), jnp.where(ok, hi, mid)

    lo0 = jnp.zeros((N_EXPERTS, 1), I32)
    hi0 = jnp.full((N_EXPERTS, 1), 0x7F800000, I32)
    thr, _ = lax.fori_loop(0, 31, bisect, (lo0, hi0))
    n_gt = jnp.sum(jnp.where(bits > thr, 1.0, 0.0), axis=1, keepdims=True)
    need = cap - n_gt

    r_i = lax.broadcasted_iota(I32, (ROUTE_BLOCK, ROUTE_BLOCK), 0)
    c_i = lax.broadcasted_iota(I32, (ROUTE_BLOCK, ROUTE_BLOCK), 1)
    tri = jnp.where(r_i < c_i, 1.0, 0.0).astype(BF16)
    lane = lax.broadcasted_iota(I32, (N_EXPERTS, MAX_ROUTE_BLOCKS), 1)

    off_ref[...] = jnp.zeros_like(off_ref)
    cnt_ref[...] = jnp.zeros_like(cnt_ref)

    def scan(j, carry):
        run_sel, run_eq = carry
        cols = pl.ds(pl.multiple_of(j * ROUTE_BLOCK, ROUTE_BLOCK), ROUTE_BLOCK)
        b = pltpu.bitcast(aff_ref[:, cols], I32)
        eq = jnp.where(b == thr, 1.0, 0.0)
        eq_rank = run_eq + _dot(eq.astype(BF16), tri)
        sel = jnp.where(b > thr, 1.0, jnp.where(eq_rank < need, eq, 0.0))
        pos = run_sel + _dot(sel.astype(BF16), tri)
        pos_ref[:, cols] = jnp.where(sel > 0.0, pos, -1.0).astype(I32)
        c = jnp.sum(sel, axis=1, keepdims=True)
        off_ref[...] = jnp.where(lane == j, run_sel.astype(I32), off_ref[...])
        cnt_ref[...] = jnp.where(lane == j, c.astype(I32), cnt_ref[...])
        return run_sel + c, run_eq + jnp.sum(eq, axis=1, keepdims=True)

    z1 = n_gt * 0.0
    lax.fori_loop(0, nblk, scan, (z1, z1))


def _route(aff_t, cap):
    n = aff_t.shape[1]
    assert n % ROUTE_BLOCK == 0 and n // ROUTE_BLOCK <= MAX_ROUTE_BLOCKS
    tbl = jax.ShapeDtypeStruct((N_EXPERTS, MAX_ROUTE_BLOCKS), I32)
    return pl.pallas_call(
        functools.partial(_route_body, cap=cap),
        out_shape=[jax.ShapeDtypeStruct((N_EXPERTS, n), I32), tbl, tbl],
        compiler_params=pltpu.CompilerParams(vmem_limit_bytes=V7X_VMEM_LIMIT_BYTES),
        name="route",
    )(aff_t)


def _dispatch_body(off_s, cnt_s, pos_ref, u_ref, xe_ref, stage_ref, xstage_ref, tail_ref, sem_ref, xsem_ref, *,
                   cap_pad):
    j = pl.program_id(0)
    last = pl.num_programs(0) - 1
    slot = j % 2

    @pl.when(j == 0)
    def _():
        tail_ref[...] = jnp.zeros_like(tail_ref)

    def geom(e, jj):
        off = off_s[e * MAX_ROUTE_BLOCKS + jj]
        c = cnt_s[e * MAX_ROUTE_BLOCKS + jj]
        a = off & (ROW_ALIGN - 1)
        return off - a, a + c, jnp.where(c > 0, (a + c + ROW_CHUNK - 1) // ROW_CHUNK, 0)

    def dst(e, row):
        return xe_ref.at[pl.ds(pl.multiple_of(e * cap_pad + row, ROW_ALIGN), ROW_CHUNK), :]

    def first_chunks(jj, sl):
        return [pltpu.make_async_copy(stage_ref.at[sl, pl.ds(e * ROW_CHUNK, ROW_CHUNK), :],
                                      dst(e, geom(e, jj)[0]), sem_ref.at[sl]) for e in range(N_EXPERTS)]

    u = u_ref[...]
    r_i = lax.broadcasted_iota(I32, (ROW_CHUNK, ROUTE_BLOCK), 0)

    def onehot(e, row0):
        return jnp.where(r_i == pos_ref[pl.ds(e, 1), :] - row0, 1.0, 0.0).astype(BF16)

    rows = _dot(jnp.concatenate([onehot(e, geom(e, j)[0]) for e in range(N_EXPERTS)], axis=0), u)
    for e in range(N_EXPERTS):
        r0 = e * ROW_CHUNK
        stage_ref[slot, r0:r0 + ROW_ALIGN, :] = (rows[r0:r0 + ROW_ALIGN] + tail_ref[e].astype(F32)).astype(BF16)
        stage_ref[slot, r0 + ROW_ALIGN:r0 + ROW_CHUNK, :] = rows[r0 + ROW_ALIGN:r0 + ROW_CHUNK].astype(BF16)

    @pl.when(j > 0)
    def _():
        for cp in first_chunks(j - 1, 1 - slot):
            cp.wait()

    for cp in first_chunks(j, slot):
        cp.start()

    def expert(e, carry):
        base, end, nch = geom(e, j)

        def extra(ch, carry):
            row0 = base + ch * ROW_CHUNK
            xstage_ref[...] = _dot(onehot(e, row0), u).astype(BF16)
            cp = pltpu.make_async_copy(xstage_ref, dst(e, row0), xsem_ref)
            cp.start()
            cp.wait()
            return carry

        lax.fori_loop(1, jnp.maximum(nch, 1), extra, carry)

        @pl.when(nch > 0)
        def _():
            part = end & (ROW_ALIGN - 1)
            r0 = pl.multiple_of((end - part) & (ROW_CHUNK - 1), ROW_ALIGN)
            keep = jnp.where(nch > 1, xstage_ref[pl.ds(r0, ROW_ALIGN), :],
                             stage_ref[slot, pl.ds(pl.multiple_of(e * ROW_CHUNK + r0, ROW_ALIGN), ROW_ALIGN), :])
            tail_ref[e] = jnp.where(part > 0, keep, jnp.zeros_like(keep))

        return carry

    lax.fori_loop(0, N_EXPERTS, expert, 0)

    @pl.when(j == last)
    def _():
        for cp in first_chunks(j, slot):
            cp.wait()
        xstage_ref[...] = jnp.zeros_like(xstage_ref)
        pads = [pltpu.make_async_copy(xstage_ref, dst(e, cap_pad - ROW_CHUNK), xsem_ref) for e in range(N_EXPERTS)]
        for cp in pads:
            cp.start()
        for cp in pads:
            cp.wait()


def _dispatch(off, cnt, pos_t, u, cap):
    n, d = u.shape
    cap_pad = cap + ROW_CHUNK
    nblk = n // ROUTE_BLOCK
    return pl.pallas_call(
        functools.partial(_dispatch_body, cap_pad=cap_pad),
        grid_spec=pltpu.PrefetchScalarGridSpec(
            num_scalar_prefetch=2,
            grid=(nblk,),
            in_specs=[pl.BlockSpec((N_EXPERTS, ROUTE_BLOCK), lambda j, *_: (0, j)),
                      pl.BlockSpec((ROUTE_BLOCK, d), lambda j, *_: (j, 0))],
            out_specs=pl.BlockSpec(memory_space=pl.ANY),
            scratch_shapes=[pltpu.VMEM((2, N_EXPERTS * ROW_CHUNK, d), BF16),
                            pltpu.VMEM((ROW_CHUNK, d), BF16),
                            pltpu.VMEM((N_EXPERTS, ROW_ALIGN, d), BF16),
                            pltpu.SemaphoreType.DMA((2,)),
                            pltpu.SemaphoreType.DMA(())]),
        out_shape=jax.ShapeDtypeStruct((N_EXPERTS * cap_pad, d), BF16),
        compiler_params=_cparams(("arbitrary",)),
        name="dispatch",
    )(off, cnt, pos_t, u)


_FFN_COLS = 512


def _ffn_body(x_ref, wg_ref, wu_ref, wd_ref, o_ref):
    x = x_ref[...]
    acc = None
    for c in range(D_EXPERT // _FFN_COLS):
        cs = slice(c * _FFN_COLS, (c + 1) * _FFN_COLS)
        g = _dot(x, wg_ref[:, cs])
        hid = (g * _sigmoid(g) * _dot(x, wu_ref[:, cs])).astype(BF16)
        part = _dot(hid, wd_ref[cs, :])
        acc = part if acc is None else acc + part
    o_ref[...] = acc.astype(BF16)


def _ffn(xe, wg, wu, wd, cap):
    e, _, d = xe.shape
    tm = 512 if cap % 512 == 0 else ROW_CHUNK
    wspec = lambda a, b: pl.BlockSpec((None, a, b), lambda ei, i: (ei, 0, 0))
    return pl.pallas_call(
        _ffn_body,
        grid=(e, cap // tm),
        in_specs=[pl.BlockSpec((None, tm, d), lambda ei, i: (ei, i, 0)),
                  wspec(d, D_EXPERT), wspec(d, D_EXPERT), wspec(D_EXPERT, d)],
        out_specs=pl.BlockSpec((None, tm, d), lambda ei, i: (ei, i, 0)),
        out_shape=jax.ShapeDtypeStruct((e, cap, d), BF16),
        compiler_params=_cparams(("parallel", "parallel")),
        name="ffn",
    )(xe, wg, wu, wd)


_GATE_TERMS = 2


def _combine_body(off_s, cnt_s, pos_ref, aff_ref, h_ref, nw_ref, ye_ref, y_ref, rows_ref, lhs_ref, xbuf_ref,
                  sem_ref, xsem_ref, acc_ref, *, cap):
    j = pl.program_id(0)
    nblk = pl.num_programs(0)
    per_e = _GATE_TERMS * ROW_CHUNK

    def geom(e, jj):
        off = off_s[e * MAX_ROUTE_BLOCKS + jj]
        c = cnt_s[e * MAX_ROUTE_BLOCKS + jj]
        a = off & (ROW_ALIGN - 1)
        return off - a, jnp.where(c > 0, (a + c + ROW_CHUNK - 1) // ROW_CHUNK, 0)

    def window(base, ch):
        lo = base + ch * ROW_CHUNK
        return lo, pl.multiple_of(jnp.minimum(lo, cap - ROW_CHUNK), ROW_ALIGN)

    def src(e, start):
        return ye_ref.at[pl.ds(pl.multiple_of(e * cap + start, ROW_ALIGN), ROW_CHUNK), :]

    def first_chunks(jj, slot):
        cps = []
        for e in range(N_EXPERTS):
            base, _ = geom(e, jj)
            _, start = window(base, 0)
            for k in range(_GATE_TERMS):
                cps.append(pltpu.make_async_copy(
                    src(e, start), rows_ref.at[slot, pl.ds(e * per_e + k * ROW_CHUNK, ROW_CHUNK), :],
                    sem_ref.at[slot]))
        return cps

    @pl.when(j == 0)
    def _():
        for cp in first_chunks(0, 0):
            cp.start()

    @pl.when(j + 1 < nblk)
    def _():
        for cp in first_chunks(j + 1, (j + 1) % 2):
            cp.start()

    def gate_terms(gate):
        g_hi = gate.astype(BF16).astype(F32)
        return g_hi, gate - g_hi

    pos_c = pos_ref[...].T
    aff_c = aff_ref[...].T
    lane = lax.broadcasted_iota(I32, (ROUTE_BLOCK, ROW_CHUNK), 1)
    for e in range(N_EXPERTS):
        base, _ = geom(e, j)
        lo, start = window(base, 0)
        r = pos_c[:, e:e + 1] - start
        hit = lane == jnp.where(r >= lo - start, r, -1)
        for k, g in enumerate(gate_terms(aff_c[:, e:e + 1])):
            lhs_ref[:, e * per_e + k * ROW_CHUNK:e * per_e + (k + 1) * ROW_CHUNK] = (
                jnp.where(hit, g, 0.0).astype(BF16))

    slot = j % 2
    for cp in first_chunks(j, slot):
        cp.wait()
    acc_ref[...] = _dot(lhs_ref[...], rows_ref[slot])

    r_i = lax.broadcasted_iota(I32, (ROW_CHUNK, ROUTE_BLOCK), 0)

    def expert(e, carry):
        base, nch = geom(e, j)

        def extra(ch, carry):
            lo, start = window(base, ch)
            cp = pltpu.make_async_copy(src(e, start), xbuf_ref, xsem_ref)
            cp.start()
            cp.wait()
            pos = pos_ref[pl.ds(e, 1), :]
            hit = (r_i == pos - start) & (pos >= lo)
            for g in gate_terms(aff_ref[pl.ds(e, 1), :]):
                acc_ref[...] += _dot_tn(jnp.where(hit, g, 0.0).astype(BF16), xbuf_ref[...])
            return carry

        return lax.fori_loop(1, jnp.maximum(nch, 1), extra, carry)

    lax.fori_loop(0, N_EXPERTS, expert, 0)
    y_ref[...] = _rms(h_ref[...] + acc_ref[...], nw_ref[...])


def _combine(off, cnt, pos_t, aff_t, h2, nw, ye, cap):
    n, d = h2.shape
    nblk = n // ROUTE_BLOCK
    eblk = pl.BlockSpec((N_EXPERTS, ROUTE_BLOCK), lambda j, *_: (0, j))
    tok = pl.BlockSpec((ROUTE_BLOCK, d), lambda j, *_: (j, 0))
    kdim = N_EXPERTS * _GATE_TERMS * ROW_CHUNK
    return pl.pallas_call(
        functools.partial(_combine_body, cap=cap),
        grid_spec=pltpu.PrefetchScalarGridSpec(
            num_scalar_prefetch=2,
            grid=(nblk,),
            in_specs=[eblk, eblk, tok, pl.BlockSpec((1, d), lambda j, *_: (0, 0)),
                      pl.BlockSpec(memory_space=pl.ANY)],
            out_specs=tok,
            scratch_shapes=[pltpu.VMEM((2, kdim, d), BF16),
                            pltpu.VMEM((ROUTE_BLOCK, kdim), BF16),
                            pltpu.VMEM((ROW_CHUNK, d), BF16),
                            pltpu.SemaphoreType.DMA((2,)),
                            pltpu.SemaphoreType.DMA(()),
                            pltpu.VMEM((ROUTE_BLOCK, d), F32)]),
        out_shape=jax.ShapeDtypeStruct((n, d), F32),
        compiler_params=_cparams(("arbitrary",)),
        name="combine",
    )(off, cnt, pos_t, aff_t, h2, nw, ye)


def _trunk(x, mem, lb, p):
    bsz, t, d = x.shape
    n = bsz * t
    cap = max(1, EC_FACTOR * n // N_EXPERTS)
    assert cap % ROW_CHUNK == 0 and t % HG_BLOCK == 0 and t % ATT_TKS == 0
    cos, s1, s2 = _rope_tables(t)
    hq, zf, hi, hg, aq, ak, av, gh, ga = _proj(x, p["norm_mix"], p["w_in"], cos, s1, s2, p["q_norm"], p["k_norm"])
    o2 = _hgrn(hq, zf, hi, lb)
    oa = _attn(p["score_bound"], aq, ak, av)
    kv = _memkv(mem, p["norm_mem"], p["w_ckv"])
    h2, u3, aff_t = _post(x, o2, hg, oa, gh, ga, kv, p["hgrn_norm"], p["w_br_hgrn"], p["w_br_attn"], p["w_out"],
                          p["norm_x"], p["w_cq"], p["w_co"], p["norm_ffn"], p["w_router_t"])
    pos_t, off, cnt = _route(aff_t, cap)
    off, cnt = off.reshape(-1), cnt.reshape(-1)
    xe = _dispatch(off, cnt, pos_t, u3.reshape(n, d), cap)
    ye = _ffn(xe.reshape(N_EXPERTS, cap + ROW_CHUNK, d), p["w_gate"], p["w_up"], p["w_down"], cap)
    y = _combine(off, cnt, pos_t, aff_t, h2.reshape(n, d), p["norm_final"], ye.reshape(N_EXPERTS * cap, d), cap)
    return y.reshape(bsz, t, d)


def kernel(x_prompt, x_sample, mem_prompt, mem_sample, norm_mix, w_in, lb_logits, hgrn_norm, q_norm, k_norm,
           w_br_hgrn, w_br_attn, w_out, norm_x, norm_mem, w_cq, w_ckv, w_co, norm_ffn, w_router, w_gate, w_up,
           w_down, norm_final):
    row = lambda v: v.reshape(1, -1).astype(F32)
    p = {
        "norm_mix": row(norm_mix[0]), "w_in": w_in[0].astype(BF16),
        "hgrn_norm": row(hgrn_norm[0]), "q_norm": row(q_norm[0]), "k_norm": row(k_norm[0]),
        "w_br_hgrn": w_br_hgrn[0].astype(BF16), "w_br_attn": w_br_attn[0].astype(BF16),
        "w_out": w_out[0].astype(BF16), "norm_x": row(norm_x[0]), "norm_mem": row(norm_mem[0]),
        "w_cq": w_cq[0].astype(BF16), "w_ckv": w_ckv[0].astype(BF16), "w_co": w_co[0].astype(BF16),
        "norm_ffn": row(norm_ffn[0]), "w_router_t": w_router[0].T.astype(BF16),
        "w_gate": w_gate[0].astype(BF16), "w_up": w_up[0].astype(BF16), "w_down": w_down[0].astype(BF16),
        "norm_final": row(norm_final), "score_bound": _score_bound(q_norm[0], k_norm[0]),
    }
    lb = jnp.cumsum(jax.nn.softmax(lb_logits.astype(F32), axis=1), axis=1)[:, 0]
    return (_trunk(x_prompt, mem_prompt, lb, p), _trunk(x_sample, mem_sample, lb, p))
```

```python
import functools

import jax
import jax.numpy as jnp
from jax import lax
from jax.experimental import pallas as pl
from jax.experimental.pallas import tpu as pltpu

F32 = jnp.float32
BF16 = jnp.bfloat16
I32 = jnp.int32

D_MODEL = 1024
EPS = 1e-6
HG_HEADS = 4
HG_D = 128
HG_W = HG_HEADS * HG_D
HG_CHUNK = 64
HG_LEAF = 16
HG_BLOCK = 512
HG_UNROLL = 2
ATT_HEADS = 8
ATT_KV = 2
ATT_GROUP = ATT_HEADS // ATT_KV
ATT_HD = 128
ATT_W = ATT_HEADS * ATT_HD
ATT_KV_W = ATT_KV * ATT_HD
GRID_W = 64
ROPE_HALF = ATT_HD // 4
ROPE_THETA = 10000.0
ATT_TQ = 256
ATT_TKS = 512
ATT_UNROLL = 4
ATT_MAX_BOUNDED_SCORE = 60.0
X_HEADS = 4
X_HD = 128
X_W = X_HEADS * X_HD
N_EXPERTS = 16
EC_FACTOR = 2
D_EXPERT = 2048
ROUTE_BLOCK = 512
ROW_CHUNK = 128
ROW_ALIGN = 16
MAX_ROUTE_BLOCKS = 128
_IN_SIZES = (HG_W, HG_W, HG_W, HG_W, HG_W, ATT_W, ATT_KV_W, ATT_KV_W, D_MODEL, D_MODEL)
_IN_OFF = tuple(sum(_IN_SIZES[:i]) for i in range(len(_IN_SIZES) + 1))
IN_WIDTH = _IN_OFF[-1]

TOKEN_TILE = 256
POST_TILE = 512
V7X_VMEM_LIMIT_BYTES = 56 * 1024 * 1024
LOG2E = 1.4426950408889634


def _cparams(sem):
    return pltpu.CompilerParams(dimension_semantics=sem, vmem_limit_bytes=V7X_VMEM_LIMIT_BYTES)


def _const_spec(shape):
    nd = len(shape)
    return pl.BlockSpec(shape, lambda *_: (0,) * nd, pipeline_mode=pl.Buffered(1))


def _rms(x, g):
    return x * lax.rsqrt(jnp.mean(x * x, axis=-1, keepdims=True) + EPS) * g


def _sigmoid(x):
    return 1.0 / (1.0 + jnp.exp(-x))


def _dot(a, b):
    return jnp.dot(a, b, preferred_element_type=F32)


def _dot_nt(a, b):
    return lax.dot_general(a, b, (((1,), (1,)), ((), ())), preferred_element_type=F32)


def _dot_tn(a, b):
    return lax.dot_general(a, b, (((0,), (0,)), ((), ())), preferred_element_type=F32)


def _proj_body(x_ref, nw_ref, w_ref, cos_ref, s1_ref, s2_ref, qg_ref, kg_ref,
               hq_ref, zf_ref, hi_ref, hg_ref, aq_ref, ak_ref, av_ref, gh_ref, ga_ref):
    u = _rms(x_ref[...], nw_ref[...]).astype(BF16)

    def mm(i):
        return _dot(u, w_ref[:, _IN_OFF[i]:_IN_OFF[i + 1]])

    hq_ref[...] = mm(0).astype(BF16)
    zf_ref[0] = mm(1)
    zf_ref[1] = mm(2)
    hi_ref[...] = mm(3).astype(BF16)
    hg_ref[...] = mm(4)
    c, s1, s2 = cos_ref[...], s1_ref[...], s2_ref[...]

    def norm_rope(z, g, scale):
        y = _rms(z, g)
        y = y * c + pltpu.roll(y, ATT_HD - ROPE_HALF, 1) * s1 + pltpu.roll(y, ROPE_HALF, 1) * s2
        return y * scale

    zq = mm(5)
    qscale = (ATT_HD ** -0.5) * LOG2E
    for h in range(ATT_HEADS):
        hs = slice(h * ATT_HD, (h + 1) * ATT_HD)
        aq_ref[h] = norm_rope(zq[:, hs], qg_ref[...], qscale).T.astype(BF16)
    zk = mm(6)
    for h in range(ATT_KV):
        hs = slice(h * ATT_HD, (h + 1) * ATT_HD)
        ak_ref[:, hs] = norm_rope(zk[:, hs], kg_ref[...], 1.0).astype(BF16)
    zv = mm(7)
    for h in range(ATT_KV):
        av_ref[h] = zv[:, h * ATT_HD:(h + 1) * ATT_HD].T.astype(BF16)
    gh_ref[...] = mm(8)
    ga_ref[...] = mm(9)


def _proj(x, nw, w_in, cos, s1, s2, qg, kg):
    bsz, t, d = x.shape
    tm = TOKEN_TILE
    tok = lambda w: pl.BlockSpec((None, tm, w), lambda b, i: (b, i, 0))
    tab = pl.BlockSpec((tm, ATT_HD), lambda b, i: (i, 0))
    sds = lambda w, dt: jax.ShapeDtypeStruct((bsz, t, w), dt)
    head_t = lambda nh: pl.BlockSpec((None, nh, ATT_HD, tm), lambda b, i: (b, 0, 0, i))
    sds_t = lambda nh: jax.ShapeDtypeStruct((bsz, nh, ATT_HD, t), BF16)
    return pl.pallas_call(
        _proj_body,
        grid=(bsz, t // tm),
        in_specs=[tok(d), _const_spec((1, d)), _const_spec((d, IN_WIDTH)), tab, tab, tab,
                  _const_spec((1, ATT_HD)), _const_spec((1, ATT_HD))],
        out_specs=[tok(HG_W),
                   pl.BlockSpec((2, None, tm, HG_W), lambda b, i: (0, b, i, 0)),
                   tok(HG_W), tok(HG_W), head_t(ATT_HEADS), tok(ATT_KV_W), head_t(ATT_KV),
                   tok(D_MODEL), tok(D_MODEL)],
        out_shape=[sds(HG_W, BF16), jax.ShapeDtypeStruct((2, bsz, t, HG_W), F32),
                   sds(HG_W, BF16), sds(HG_W, F32), sds_t(ATT_HEADS), sds(ATT_KV_W, BF16),
                   sds_t(ATT_KV), sds(D_MODEL, F32), sds(D_MODEL, F32)],
        compiler_params=_cparams(("parallel", "parallel")),
        name="proj",
    )(x, nw, w_in, cos, s1, s2, qg, kg)


def _rope_tables(t):
    pos_t = jnp.arange(t, dtype=I32)
    pos = jnp.stack([pos_t // GRID_W, pos_t % GRID_W], axis=-1).astype(F32)
    inv = jnp.power(ROPE_THETA, -jnp.arange(0, 2 * ROPE_HALF, 2, dtype=F32) / (2 * ROPE_HALF))
    ang = pos[:, :, None] * inv
    cos, sin = jnp.cos(ang), jnp.sin(ang)
    zero = jnp.zeros_like(sin)
    c = jnp.concatenate([cos, cos], axis=-1).reshape(t, ATT_HD)
    s1 = jnp.concatenate([-sin, zero], axis=-1).reshape(t, ATT_HD)
    s2 = jnp.concatenate([zero, sin], axis=-1).reshape(t, ATT_HD)
    return c, s1, s2


def _split3(x):
    a = x.astype(BF16)
    r = x - a.astype(F32)
    b = r.astype(BF16)
    c = (r - b.astype(F32)).astype(BF16)
    return a, b, c


def _hgrn_chunks(qs, zs, vs, lbs, sts, rev):
    C, LS, SUB = HG_CHUNK, HG_LEAF, 8
    nb = C // LS
    heads = range(len(qs))
    r_i = lax.broadcasted_iota(I32, (C, C), 0)
    c_i = lax.broadcasted_iota(I32, (C, C), 1)
    before = (c_i >= r_i) if rev else (c_i <= r_i)
    tri = jnp.where(before, 1.0, 0.0).astype(BF16)

    front = []
    for h in heads:
        f = lbs[h] + (1.0 - lbs[h]) * _sigmoid(zs[h])
        kk = 1.0 - f
        g1, g2, g3 = _split3(jnp.log(f))
        b2 = (_dot(tri, g1) + _dot(tri, g2) + _dot(tri, g3)) * LOG2E
        front.append((qs[h].astype(F32), kk, b2, b2[0:1] if rev else b2[C - 1:C]))

    inter = []
    for h in heads:
        qf, kk, b2, btot = front[h]
        o = _dot_nt((qf * jnp.exp2(b2)).astype(BF16), sts[h].astype(BF16))
        kd = (kk * jnp.exp2(btot - b2)).astype(BF16)
        inter.append((o, sts[h] * jnp.exp2(btot) + _dot_tn(vs[h], kd)))

    cols = range(1, nb) if rev else range(nb - 1)
    zeros_l = jnp.zeros((LS, HG_D), BF16)
    zero_row = jnp.zeros((LS, HG_D * (nb - 1)), BF16)
    a_off = []
    for h in heads:
        qf, kk, b2, _ = front[h]
        lhs, rhs_rows = [], {}
        for p, j in enumerate(cols):
            rr = LS * j if rev else LS * (j + 1) - 1
            ref = b2[rr:rr + 1]
            rows = slice(0, LS * j) if rev else slice(LS * (j + 1), C)
            scaled = (qf[rows] * jnp.exp2(b2[rows] - ref)).astype(BF16)
            pad = jnp.zeros((C - scaled.shape[0], HG_D), BF16)
            lhs.append(jnp.concatenate([scaled, pad] if rev else [pad, scaled], axis=0))
            kj = (kk[LS * j:LS * (j + 1)] * jnp.exp2(ref - b2[LS * j:LS * (j + 1)])).astype(BF16)
            rhs_rows[j] = jnp.concatenate([kj if i == p else zeros_l for i in range(nb - 1)], axis=1)
        kmat = jnp.concatenate([rhs_rows.get(j, zero_row) for j in range(nb)], axis=0)
        a_off.append(_dot_nt(jnp.concatenate(lhs, axis=1), kmat))

    lane = lax.broadcasted_iota(I32, (SUB, C), 1)
    a_diag = []
    for h in heads:
        qf, kk, b2, _ = front[h]
        tiles = [jnp.zeros((SUB, C), F32) for _ in range(C // SUB)]
        for s in range(C):
            blk = s // LS
            first, last = (LS * blk // SUB, s // SUB) if rev else (s // SUB, LS * (blk + 1) // SUB - 1)
            brow, krow = b2[s:s + 1], kk[s:s + 1]
            for t in range(first, last + 1):
                rs = slice(SUB * t, SUB * (t + 1))
                d = b2[rs] - brow
                if t == s // SUB:
                    d = jnp.minimum(d, 0.0)
                col = jnp.sum(qf[rs] * jnp.exp2(d) * krow, axis=1, keepdims=True)
                tiles[t] = jnp.where(lane == s, col, tiles[t])
        a_diag.append(jnp.where((r_i <= c_i) if rev else (r_i >= c_i), jnp.concatenate(tiles, axis=0), 0.0))

    return [(inter[h][0] + _dot((a_off[h] + a_diag[h]).astype(BF16), vs[h]), inter[h][1]) for h in heads]


def _hgrn_body(q_ref, z_ref, v_ref, lb_ref, o_ref, st_ref):
    d = pl.program_id(1)
    j = pl.program_id(2)
    nch = HG_BLOCK // HG_CHUNK
    hss = [slice(h * HG_D, (h + 1) * HG_D) for h in range(HG_HEADS)]

    @pl.when(j == 0)
    def _():
        st_ref[...] = jnp.zeros_like(st_ref)

    def run(rev):
        def step(i, carry):
            c = (nch - 1 - i) if rev else i
            rows = pl.ds(pl.multiple_of(c * HG_CHUNK, HG_CHUNK), HG_CHUNK)
            outs = _hgrn_chunks([q_ref[rows, hs] for hs in hss], [z_ref[rows, hs] for hs in hss],
                                [v_ref[rows, hs] for hs in hss],
                                [lb_ref[int(rev):int(rev) + 1, hs] for hs in hss],
                                [st_ref[h] for h in range(HG_HEADS)], rev)
            for h, (o, st) in enumerate(outs):
                o_ref[rows, hss[h]] = o
                st_ref[h] = st
            return carry
        lax.fori_loop(0, nch, step, 0, unroll=HG_UNROLL)

    @pl.when(d == 0)
    def _():
        run(False)

    @pl.when(d == 1)
    def _():
        run(True)


def _hgrn(hq, zf, hi, lb):
    bsz, t, _ = hq.shape
    nblk = t // HG_BLOCK

    def blk(d, j):
        return jnp.where(d == 0, j, nblk - 1 - j)

    tok = pl.BlockSpec((None, HG_BLOCK, HG_W), lambda b, d, j: (b, blk(d, j), 0))
    dirtok = pl.BlockSpec((None, None, HG_BLOCK, HG_W), lambda b, d, j: (d, b, blk(d, j), 0))
    return pl.pallas_call(
        _hgrn_body,
        grid=(bsz, 2, nblk),
        in_specs=[tok, dirtok, tok, _const_spec((2, HG_W))],
        out_specs=dirtok,
        out_shape=jax.ShapeDtypeStruct((2, bsz, t, HG_W), F32),
        scratch_shapes=[pltpu.VMEM((HG_HEADS, HG_D, HG_D), F32)],
        compiler_params=_cparams(("parallel", "arbitrary", "arbitrary")),
        name="hgrn",
    )(hq, zf, hi, lb)


def _attn_body(bound_s, q_ref, k_ref, v_ref, o_ref, qt_ref, s0_ref, m_ref, l_ref, acc_ref, *, unroll, bounded):
    tq, tks = ATT_TQ, ATT_TKS
    nsub = k_ref.shape[0] // tks
    for g in range(ATT_GROUP):
        qt_ref[:, g * tq:(g + 1) * tq] = q_ref[g]
    m_ref[...] = jnp.full_like(m_ref, -jnp.inf)
    l_ref[...] = jnp.zeros_like(l_ref)
    acc_ref[...] = jnp.zeros_like(acc_ref)
    qt = qt_ref[...]
    s0_ref[...] = _dot(k_ref[0:tks, :], qt)
    bound = bound_s[0]

    def group(i, carry):
        m, l, acc = m_ref[...], l_ref[...], acc_ref[...]
        s_next = s0_ref[...]
        for u in range(unroll):
            c = i * unroll + u
            s = s_next
            nxt = jnp.minimum(c + 1, nsub - 1)
            s_next = _dot(k_ref[pl.ds(pl.multiple_of(nxt * tks, tks), tks), :], qt)
            vt = v_ref[:, pl.ds(pl.multiple_of(c * tks, tks), tks)]
            if bounded:
                p = jnp.exp2(s - bound)
                l = l + jnp.sum(p, axis=0, keepdims=True)
                acc = acc + _dot(vt, p.astype(BF16))
            else:
                m_new = jnp.maximum(m, jnp.max(s, axis=0, keepdims=True))
                alpha = jnp.exp2(m - m_new)
                p = jnp.exp2(s - m_new)
                l = alpha * l + jnp.sum(p, axis=0, keepdims=True)
                acc = alpha * acc + _dot(vt, p.astype(BF16))
                m = m_new
        s0_ref[...] = s_next
        m_ref[...], l_ref[...], acc_ref[...] = m, l, acc
        return carry

    lax.fori_loop(0, nsub // unroll, group, 0)
    out = acc_ref[...] / l_ref[...]
    for g in range(ATT_GROUP):
        o_ref[:, g * ATT_HD:(g + 1) * ATT_HD] = out[:, g * tq:(g + 1) * tq].T.astype(BF16)


def _attn_call(bound, aq_t, ak, av_t, bounded):
    bsz, _, _, t = aq_t.shape
    tq = ATT_TQ
    nsub = t // ATT_TKS
    unroll = ATT_UNROLL if nsub % ATT_UNROLL == 0 else 1
    gw = ATT_GROUP * ATT_HD
    return pl.pallas_call(
        functools.partial(_attn_body, unroll=unroll, bounded=bounded),
        grid_spec=pltpu.PrefetchScalarGridSpec(
            num_scalar_prefetch=1,
            grid=(bsz, ATT_KV, t // tq),
            in_specs=[pl.BlockSpec((None, ATT_GROUP, ATT_HD, tq), lambda b, h, qi, *_: (b, h, 0, qi)),
                      pl.BlockSpec((None, t, ATT_HD), lambda b, h, qi, *_: (b, 0, h)),
                      pl.BlockSpec((None, None, ATT_HD, t), lambda b, h, qi, *_: (b, h, 0, 0))],
            out_specs=pl.BlockSpec((None, tq, gw), lambda b, h, qi, *_: (b, qi, h)),
            scratch_shapes=[pltpu.VMEM((ATT_HD, ATT_GROUP * tq), BF16),
                            pltpu.VMEM((ATT_TKS, ATT_GROUP * tq), F32),
                            pltpu.VMEM((1, ATT_GROUP * tq), F32),
                            pltpu.VMEM((1, ATT_GROUP * tq), F32),
                            pltpu.VMEM((ATT_HD, ATT_GROUP * tq), F32)]),
        out_shape=jax.ShapeDtypeStruct((bsz, t, ATT_W), BF16),
        compiler_params=_cparams(("parallel", "parallel", "arbitrary")),
        name="attn_bounded" if bounded else "attn",
    )(bound, aq_t, ak, av_t)


def _attn(bound, aq_t, ak, av_t):
    return lax.cond(bound[0] <= ATT_MAX_BOUNDED_SCORE,
                    functools.partial(_attn_call, bounded=True),
                    functools.partial(_attn_call, bounded=False),
                    bound, aq_t, ak, av_t)


def _score_bound(q_gain, k_gain):
    qscale = (ATT_HD ** -0.5) * LOG2E
    b = 1.01 * ATT_HD * qscale * jnp.max(jnp.abs(q_gain)) * jnp.max(jnp.abs(k_gain))
    return b.reshape(1).astype(F32)


def _memkv_body(m_ref, nw_ref, w_ref, o_ref):
    o_ref[...] = _dot(_rms(m_ref[...], nw_ref[...]).astype(BF16), w_ref[...]).astype(BF16)


def _memkv(mem, nw, w_ckv):
    bsz, m, d = mem.shape
    return pl.pallas_call(
        _memkv_body,
        grid=(bsz,),
        in_specs=[pl.BlockSpec((None, m, d), lambda b: (b, 0, 0)), _const_spec((1, d)),
                  _const_spec((d, 2 * X_W))],
        out_specs=pl.BlockSpec((None, m, 2 * X_W), lambda b: (b, 0, 0)),
        out_shape=jax.ShapeDtypeStruct((bsz, m, 2 * X_W), BF16),
        compiler_params=_cparams(("parallel",)),
        name="memkv",
    )(mem, nw, w_ckv)


def _post_body(x_ref, o2_ref, hg_ref, oa_ref, gh_ref, ga_ref, kv_ref,
               hgn_ref, wbh_ref, wba_ref, wo_ref, nx_ref, wcq_ref, wco_ref, nf_ref, wr_ref,
               h2_ref, u3_ref, aff_ref):
    o = o2_ref[0] + o2_ref[1]
    parts = []
    for h in range(HG_HEADS):
        oh = o[:, h * HG_D:(h + 1) * HG_D]
        parts.append(oh * lax.rsqrt(jnp.mean(oh * oh, axis=-1, keepdims=True) + EPS))
    hgv = hg_ref[...]
    oh = (jnp.concatenate(parts, axis=1) * hgn_ref[...] * (hgv * _sigmoid(hgv))).astype(BF16)
    merged = (_sigmoid(gh_ref[...]) * _dot(oh, wbh_ref[...])
              + _sigmoid(ga_ref[...]) * _dot(oa_ref[...], wba_ref[...]))
    h1 = x_ref[...] + _dot(merged.astype(BF16), wo_ref[...])

    u2 = _rms(h1, nx_ref[...]).astype(BF16)
    qx = (_dot(u2, wcq_ref[...]) * (X_HD ** -0.5)).astype(BF16)
    outs = []
    for h in range(X_HEADS):
        hs = slice(h * X_HD, (h + 1) * X_HD)
        s = _dot_nt(qx[:, hs], kv_ref[:, hs])
        p = jnp.exp(s - jnp.max(s, axis=1, keepdims=True))
        ov = _dot(p.astype(BF16), kv_ref[:, X_W + h * X_HD:X_W + (h + 1) * X_HD])
        outs.append(ov / jnp.sum(p, axis=1, keepdims=True))
    h2 = h1 + _dot(jnp.concatenate(outs, axis=1).astype(BF16), wco_ref[...])
    h2_ref[...] = h2

    u3 = _rms(h2, nf_ref[...]).astype(BF16)
    u3_ref[...] = u3
    logits = _dot_nt(wr_ref[...], u3)
    e = jnp.exp(logits - jnp.max(logits, axis=0, keepdims=True))
    aff_ref[...] = e / jnp.sum(e, axis=0, keepdims=True)


def _post(x, o2, hg, oa, gh, ga, kv, hgn, wbh, wba, wo, nx, wcq, wco, nf, wr_t):
    bsz, t, d = x.shape
    tm = POST_TILE
    nt = t // tm
    m = kv.shape[1]
    tok = lambda w: pl.BlockSpec((None, tm, w), lambda b, i: (b, i, 0))
    return pl.pallas_call(
        _post_body,
        grid=(bsz, nt),
        in_specs=[tok(d), pl.BlockSpec((2, None, tm, HG_W), lambda b, i: (0, b, i, 0)), tok(HG_W),
                  tok(ATT_W), tok(d), tok(d), pl.BlockSpec((None, m, 2 * X_W), lambda b, i: (b, 0, 0)),
                  _const_spec((1, HG_W)), _const_spec((HG_W, d)), _const_spec((ATT_W, d)),
                  _const_spec((d, d)), _const_spec((1, d)), _const_spec((d, X_W)), _const_spec((X_W, d)),
                  _const_spec((1, d)), _const_spec((N_EXPERTS, d))],
        out_specs=[tok(d), tok(d), pl.BlockSpec((N_EXPERTS, tm), lambda b, i: (0, b * nt + i))],
        out_shape=[jax.ShapeDtypeStruct((bsz, t, d), F32), jax.ShapeDtypeStruct((bsz, t, d), BF16),
                   jax.ShapeDtypeStruct((N_EXPERTS, bsz * t), F32)],
        compiler_params=_cparams(("parallel", "parallel")),
        name="post",
    )(x, o2, hg, oa, gh, ga, kv, hgn, wbh, wba, wo, nx, wcq, wco, nf, wr_t)


def _route_body(aff_ref, pos_ref, off_ref, cnt_ref, *, cap):
    n = aff_ref.shape[1]
    nblk = n // ROUTE_BLOCK
    bits = pltpu.bitcast(aff_ref[...], I32)

    def count_ge(th):
        return jnp.sum(jnp.where(bits >= th, 1.0, 0.0), axis=1, keepdims=True)

    def bisect(_, lohi):
        lo, hi = lohi
        mid = lo + lax.shift_right_logical(hi - lo, 1)
        ok = count_ge(mid) >= cap
        return jnp.where(ok, mid, lo), jnp.where(ok, hi, mid)

    lo0 = jnp.zeros((N_EXPERTS, 1), I32)
    hi0 = jnp.full((N_EXPERTS, 1), 0x7F800000, I32)
    thr, _ = lax.fori_loop(0, 31, bisect, (lo0, hi0))
    n_gt = jnp.sum(jnp.where(bits > thr, 1.0, 0.0), axis=1, keepdims=True)
    need = cap - n_gt

    r_i = lax.broadcasted_iota(I32, (ROUTE_BLOCK, ROUTE_BLOCK), 0)
    c_i = lax.broadcasted_iota(I32, (ROUTE_BLOCK, ROUTE_BLOCK), 1)
    tri = jnp.where(r_i < c_i, 1.0, 0.0).astype(BF16)
    lane = lax.broadcasted_iota(I32, (N_EXPERTS, MAX_ROUTE_BLOCKS), 1)

    off_ref[...] = jnp.zeros_like(off_ref)
    cnt_ref[...] = jnp.zeros_like(cnt_ref)

    def scan(j, carry):
        run_sel, run_eq = carry
        cols = pl.ds(pl.multiple_of(j * ROUTE_BLOCK, ROUTE_BLOCK), ROUTE_BLOCK)
        b = pltpu.bitcast(aff_ref[:, cols], I32)
        eq = jnp.where(b == thr, 1.0, 0.0)
        eq_rank = run_eq + _dot(eq.astype(BF16), tri)
        sel = jnp.where(b > thr, 1.0, jnp.where(eq_rank < need, eq, 0.0))
        pos = run_sel + _dot(sel.astype(BF16), tri)
        pos_ref[:, cols] = jnp.where(sel > 0.0, pos, -1.0).astype(I32)
        c = jnp.sum(sel, axis=1, keepdims=True)
        off_ref[...] = jnp.where(lane == j, run_sel.astype(I32), off_ref[...])
        cnt_ref[...] = jnp.where(lane == j, c.astype(I32), cnt_ref[...])
        return run_sel + c, run_eq + jnp.sum(eq, axis=1, keepdims=True)

    z1 = n_gt * 0.0
    lax.fori_loop(0, nblk, scan, (z1, z1))


def _route(aff_t, cap):
    n = aff_t.shape[1]
    assert n % ROUTE_BLOCK == 0 and n // ROUTE_BLOCK <= MAX_ROUTE_BLOCKS
    tbl = jax.ShapeDtypeStruct((N_EXPERTS, MAX_ROUTE_BLOCKS), I32)
    return pl.pallas_call(
        functools.partial(_route_body, cap=cap),
        out_shape=[jax.ShapeDtypeStruct((N_EXPERTS, n), I32), tbl, tbl],
        compiler_params=pltpu.CompilerParams(vmem_limit_bytes=V7X_VMEM_LIMIT_BYTES),
        name="route",
    )(aff_t)


def _dispatch_body(off_s, cnt_s, pos_ref, u_ref, xe_ref, stage_ref, xstage_ref, tail_ref, sem_ref, xsem_ref, *,
                   cap_pad):
    j = pl.program_id(0)
    last = pl.num_programs(0) - 1
    slot = j % 2

    @pl.when(j == 0)
    def _():
        tail_ref[...] = jnp.zeros_like(tail_ref)

    def geom(e, jj):
        off = off_s[e * MAX_ROUTE_BLOCKS + jj]
        c = cnt_s[e * MAX_ROUTE_BLOCKS + jj]
        a = off & (ROW_ALIGN - 1)
        return off - a, a + c, jnp.where(c > 0, (a + c + ROW_CHUNK - 1) // ROW_CHUNK, 0)

    def dst(e, row):
        return xe_ref.at[pl.ds(pl.multiple_of(e * cap_pad + row, ROW_ALIGN), ROW_CHUNK), :]

    def first_chunks(jj, sl):
        return [pltpu.make_async_copy(stage_ref.at[sl, pl.ds(e * ROW_CHUNK, ROW_CHUNK), :],
                                      dst(e, geom(e, jj)[0]), sem_ref.at[sl]) for e in range(N_EXPERTS)]

    u = u_ref[...]
    r_i = lax.broadcasted_iota(I32, (ROW_CHUNK, ROUTE_BLOCK), 0)

    def onehot(e, row0):
        return jnp.where(r_i == pos_ref[pl.ds(e, 1), :] - row0, 1.0, 0.0).astype(BF16)

    rows = _dot(jnp.concatenate([onehot(e, geom(e, j)[0]) for e in range(N_EXPERTS)], axis=0), u)
    for e in range(N_EXPERTS):
        r0 = e * ROW_CHUNK
        stage_ref[slot, r0:r0 + ROW_ALIGN, :] = (rows[r0:r0 + ROW_ALIGN] + tail_ref[e].astype(F32)).astype(BF16)
        stage_ref[slot, r0 + ROW_ALIGN:r0 + ROW_CHUNK, :] = rows[r0 + ROW_ALIGN:r0 + ROW_CHUNK].astype(BF16)

    @pl.when(j > 0)
    def _():
        for cp in first_chunks(j - 1, 1 - slot):
            cp.wait()

    for cp in first_chunks(j, slot):
        cp.start()

    def expert(e, carry):
        base, end, nch = geom(e, j)

        def extra(ch, carry):
            row0 = base + ch * ROW_CHUNK
            xstage_ref[...] = _dot(onehot(e, row0), u).astype(BF16)
            cp = pltpu.make_async_copy(xstage_ref, dst(e, row0), xsem_ref)
            cp.start()
            cp.wait()
            return carry

        lax.fori_loop(1, jnp.maximum(nch, 1), extra, carry)

        @pl.when(nch > 0)
        def _():
            part = end & (ROW_ALIGN - 1)
            r0 = pl.multiple_of((end - part) & (ROW_CHUNK - 1), ROW_ALIGN)
            keep = jnp.where(nch > 1, xstage_ref[pl.ds(r0, ROW_ALIGN), :],
                             stage_ref[slot, pl.ds(pl.multiple_of(e * ROW_CHUNK + r0, ROW_ALIGN), ROW_ALIGN), :])
            tail_ref[e] = jnp.where(part > 0, keep, jnp.zeros_like(keep))

        return carry

    lax.fori_loop(0, N_EXPERTS, expert, 0)

    @pl.when(j == last)
    def _():
        for cp in first_chunks(j, slot):
            cp.wait()
        xstage_ref[...] = jnp.zeros_like(xstage_ref)
        pads = [pltpu.make_async_copy(xstage_ref, dst(e, cap_pad - ROW_CHUNK), xsem_ref) for e in range(N_EXPERTS)]
        for cp in pads:
            cp.start()
        for cp in pads:
            cp.wait()


def _dispatch(off, cnt, pos_t, u, cap):
    n, d = u.shape
    cap_pad = cap + ROW_CHUNK
    nblk = n // ROUTE_BLOCK
    return pl.pallas_call(
        functools.partial(_dispatch_body, cap_pad=cap_pad),
        grid_spec=pltpu.PrefetchScalarGridSpec(
            num_scalar_prefetch=2,
            grid=(nblk,),
            in_specs=[pl.BlockSpec((N_EXPERTS, ROUTE_BLOCK), lambda j, *_: (0, j)),
                      pl.BlockSpec((ROUTE_BLOCK, d), lambda j, *_: (j, 0))],
            out_specs=pl.BlockSpec(memory_space=pl.ANY),
            scratch_shapes=[pltpu.VMEM((2, N_EXPERTS * ROW_CHUNK, d), BF16),
                            pltpu.VMEM((ROW_CHUNK, d), BF16),
                            pltpu.VMEM((N_EXPERTS, ROW_ALIGN, d), BF16),
                            pltpu.SemaphoreType.DMA((2,)),
                            pltpu.SemaphoreType.DMA(())]),
        out_shape=jax.ShapeDtypeStruct((N_EXPERTS * cap_pad, d), BF16),
        compiler_params=_cparams(("arbitrary",)),
        name="dispatch",
    )(off, cnt, pos_t, u)


_FFN_COLS = 512


def _ffn_body(x_ref, wg_ref, wu_ref, wd_ref, o_ref):
    x = x_ref[...]
    acc = None
    for c in range(D_EXPERT // _FFN_COLS):
        cs = slice(c * _FFN_COLS, (c + 1) * _FFN_COLS)
        g = _dot(x, wg_ref[:, cs])
        hid = (g * _sigmoid(g) * _dot(x, wu_ref[:, cs])).astype(BF16)
        part = _dot(hid, wd_ref[cs, :])
        acc = part if acc is None else acc + part
    o_ref[...] = acc.astype(BF16)


def _ffn(xe, wg, wu, wd, cap):
    e, _, d = xe.shape
    tm = 512 if cap % 512 == 0 else ROW_CHUNK
    wspec = lambda a, b: pl.BlockSpec((None, a, b), lambda ei, i: (ei, 0, 0))
    return pl.pallas_call(
        _ffn_body,
        grid=(e, cap // tm),
        in_specs=[pl.BlockSpec((None, tm, d), lambda ei, i: (ei, i, 0)),
                  wspec(d, D_EXPERT), wspec(d, D_EXPERT), wspec(D_EXPERT, d)],
        out_specs=pl.BlockSpec((None, tm, d), lambda ei, i: (ei, i, 0)),
        out_shape=jax.ShapeDtypeStruct((e, cap, d), BF16),
        compiler_params=_cparams(("parallel", "parallel")),
        name="ffn",
    )(xe, wg, wu, wd)


_GATE_TERMS = 2


def _combine_body(off_s, cnt_s, pos_ref, aff_ref, h_ref, nw_ref, ye_ref, y_ref, rows_ref, lhs_ref, xbuf_ref,
                  sem_ref, xsem_ref, acc_ref, *, cap):
    j = pl.program_id(0)
    nblk = pl.num_programs(0)
    per_e = _GATE_TERMS * ROW_CHUNK

    def geom(e, jj):
        off = off_s[e * MAX_ROUTE_BLOCKS + jj]
        c = cnt_s[e * MAX_ROUTE_BLOCKS + jj]
        a = off & (ROW_ALIGN - 1)
        return off - a, jnp.where(c > 0, (a + c + ROW_CHUNK - 1) // ROW_CHUNK, 0)

    def window(base, ch):
        lo = base + ch * ROW_CHUNK
        return lo, pl.multiple_of(jnp.minimum(lo, cap - ROW_CHUNK), ROW_ALIGN)

    def src(e, start):
        return ye_ref.at[pl.ds(pl.multiple_of(e * cap + start, ROW_ALIGN), ROW_CHUNK), :]

    def first_chunks(jj, slot):
        cps = []
        for e in range(N_EXPERTS):
            base, _ = geom(e, jj)
            _, start = window(base, 0)
            for k in range(_GATE_TERMS):
                cps.append(pltpu.make_async_copy(
                    src(e, start), rows_ref.at[slot, pl.ds(e * per_e + k * ROW_CHUNK, ROW_CHUNK), :],
                    sem_ref.at[slot]))
        return cps

    @pl.when(j == 0)
    def _():
        for cp in first_chunks(0, 0):
            cp.start()

    @pl.when(j + 1 < nblk)
    def _():
        for cp in first_chunks(j + 1, (j + 1) % 2):
            cp.start()

    def gate_terms(gate):
        g_hi = gate.astype(BF16).astype(F32)
        return g_hi, gate - g_hi

    pos_c = pos_ref[...].T
    aff_c = aff_ref[...].T
    lane = lax.broadcasted_iota(I32, (ROUTE_BLOCK, ROW_CHUNK), 1)
    for e in range(N_EXPERTS):
        base, _ = geom(e, j)
        lo, start = window(base, 0)
        r = pos_c[:, e:e + 1] - start
        hit = lane == jnp.where(r >= lo - start, r, -1)
        for k, g in enumerate(gate_terms(aff_c[:, e:e + 1])):
            lhs_ref[:, e * per_e + k * ROW_CHUNK:e * per_e + (k + 1) * ROW_CHUNK] = (
                jnp.where(hit, g, 0.0).astype(BF16))

    slot = j % 2
    for cp in first_chunks(j, slot):
        cp.wait()
    acc_ref[...] = _dot(lhs_ref[...], rows_ref[slot])

    r_i = lax.broadcasted_iota(I32, (ROW_CHUNK, ROUTE_BLOCK), 0)

    def expert(e, carry):
        base, nch = geom(e, j)

        def extra(ch, carry):
            lo, start = window(base, ch)
            cp = pltpu.make_async_copy(src(e, start), xbuf_ref, xsem_ref)
            cp.start()
            cp.wait()
            pos = pos_ref[pl.ds(e, 1), :]
            hit = (r_i == pos - start) & (pos >= lo)
            for g in gate_terms(aff_ref[pl.ds(e, 1), :]):
                acc_ref[...] += _dot_tn(jnp.where(hit, g, 0.0).astype(BF16), xbuf_ref[...])
            return carry

        return lax.fori_loop(1, jnp.maximum(nch, 1), extra, carry)

    lax.fori_loop(0, N_EXPERTS, expert, 0)
    y_ref[...] = _rms(h_ref[...] + acc_ref[...], nw_ref[...])


def _combine(off, cnt, pos_t, aff_t, h2, nw, ye, cap):
    n, d = h2.shape
    nblk = n // ROUTE_BLOCK
    eblk = pl.BlockSpec((N_EXPERTS, ROUTE_BLOCK), lambda j, *_: (0, j))
    tok = pl.BlockSpec((ROUTE_BLOCK, d), lambda j, *_: (j, 0))
    kdim = N_EXPERTS * _GATE_TERMS * ROW_CHUNK
    return pl.pallas_call(
        functools.partial(_combine_body, cap=cap),
        grid_spec=pltpu.PrefetchScalarGridSpec(
            num_scalar_prefetch=2,
            grid=(nblk,),
            in_specs=[eblk, eblk, tok, pl.BlockSpec((1, d), lambda j, *_: (0, 0)),
                      pl.BlockSpec(memory_space=pl.ANY)],
            out_specs=tok,
            scratch_shapes=[pltpu.VMEM((2, kdim, d), BF16),
                            pltpu.VMEM((ROUTE_BLOCK, kdim), BF16),
                            pltpu.VMEM((ROW_CHUNK, d), BF16),
                            pltpu.SemaphoreType.DMA((2,)),
                            pltpu.SemaphoreType.DMA(()),
                            pltpu.VMEM((ROUTE_BLOCK, d), F32)]),
        out_shape=jax.ShapeDtypeStruct((n, d), F32),
        compiler_params=_cparams(("arbitrary",)),
        name="combine",
    )(off, cnt, pos_t, aff_t, h2, nw, ye)


def _trunk(x, mem, lb, p):
    bsz, t, d = x.shape
    n = bsz * t
    cap = max(1, EC_FACTOR * n // N_EXPERTS)
    assert cap % ROW_CHUNK == 0 and t % HG_BLOCK == 0 and t % ATT_TKS == 0
    cos, s1, s2 = _rope_tables(t)
    hq, zf, hi, hg, aq, ak, av, gh, ga = _proj(x, p["norm_mix"], p["w_in"], cos, s1, s2, p["q_norm"], p["k_norm"])
    o2 = _hgrn(hq, zf, hi, lb)
    oa = _attn(p["score_bound"], aq, ak, av)
    kv = _memkv(mem, p["norm_mem"], p["w_ckv"])
    h2, u3, aff_t = _post(x, o2, hg, oa, gh, ga, kv, p["hgrn_norm"], p["w_br_hgrn"], p["w_br_attn"], p["w_out"],
                          p["norm_x"], p["w_cq"], p["w_co"], p["norm_ffn"], p["w_router_t"])
    pos_t, off, cnt = _route(aff_t, cap)
    off, cnt = off.reshape(-1), cnt.reshape(-1)
    xe = _dispatch(off, cnt, pos_t, u3.reshape(n, d), cap)
    ye = _ffn(xe.reshape(N_EXPERTS, cap + ROW_CHUNK, d), p["w_gate"], p["w_up"], p["w_down"], cap)
    y = _combine(off, cnt, pos_t, aff_t, h2.reshape(n, d), p["norm_final"], ye.reshape(N_EXPERTS * cap, d), cap)
    return y.reshape(bsz, t, d)


def kernel(x_prompt, x_sample, mem_prompt, mem_sample, norm_mix, w_in, lb_logits, hgrn_norm, q_norm, k_norm,
           w_br_hgrn, w_br_attn, w_out, norm_x, norm_mem, w_cq, w_ckv, w_co, norm_ffn, w_router, w_gate, w_up,
           w_down, norm_final):
    row = lambda v: v.reshape(1, -1).astype(F32)
    p = {
        "norm_mix": row(norm_mix[0]), "w_in": w_in[0].astype(BF16),
        "hgrn_norm": row(hgrn_norm[0]), "q_norm": row(q_norm[0]), "k_norm": row(k_norm[0]),
        "w_br_hgrn": w_br_hgrn[0].astype(BF16), "w_br_attn": w_br_attn[0].astype(BF16),
        "w_out": w_out[0].astype(BF16), "norm_x": row(norm_x[0]), "norm_mem": row(norm_mem[0]),
        "w_cq": w_cq[0].astype(BF16), "w_ckv": w_ckv[0].astype(BF16), "w_co": w_co[0].astype(BF16),
        "norm_ffn": row(norm_ffn[0]), "w_router_t": w_router[0].T.astype(BF16),
        "w_gate": w_gate[0].astype(BF16), "w_up": w_up[0].astype(BF16), "w_down": w_down[0].astype(BF16),
        "norm_final": row(norm_final), "score_bound": _score_bound(q_norm[0], k_norm[0]),
    }
    lb = jnp.cumsum(jax.nn.softmax(lb_logits.astype(F32), axis=1), axis=1)[:, 0]
    return (_trunk(x_prompt, mem_prompt, lb, p), _trunk(x_sample, mem_sample, lb, p))
```

```python
import functools

import jax
import jax.numpy as jnp
from jax import lax
from jax.experimental import pallas as pl
from jax.experimental.pallas import tpu as pltpu

F32 = jnp.float32
BF16 = jnp.bfloat16
I32 = jnp.int32

D_MODEL = 1024
EPS = 1e-6
HG_HEADS = 4
HG_D = 128
HG_W = HG_HEADS * HG_D
HG_CHUNK = 64
HG_LEAF = 16
HG_BLOCK = 512
HG_UNROLL = 4
ATT_HEADS = 8
ATT_KV = 2
ATT_GROUP = ATT_HEADS // ATT_KV
ATT_HD = 128
ATT_W = ATT_HEADS * ATT_HD
ATT_KV_W = ATT_KV * ATT_HD
GRID_W = 64
ROPE_HALF = ATT_HD // 4
ROPE_THETA = 10000.0
ATT_TQ = 256
ATT_TKS = 512
ATT_UNROLL = 4
ATT_MAX_BOUNDED_SCORE = 60.0
X_HEADS = 4
X_HD = 128
X_W = X_HEADS * X_HD
N_EXPERTS = 16
EC_FACTOR = 2
D_EXPERT = 2048
ROUTE_BLOCK = 512
ROW_CHUNK = 128
ROW_ALIGN = 16
MAX_ROUTE_BLOCKS = 128
GATE_TERMS = 3
GATE_LANES = 128
_IN_SIZES = (HG_W, HG_W, HG_W, HG_W, HG_W, ATT_W, ATT_KV_W, ATT_KV_W, D_MODEL, D_MODEL)
_IN_OFF = tuple(sum(_IN_SIZES[:i]) for i in range(len(_IN_SIZES) + 1))
IN_WIDTH = _IN_OFF[-1]

TOKEN_TILE = 256
POST_TILE = 512
V7X_VMEM_LIMIT_BYTES = 56 * 1024 * 1024
LOG2E = 1.4426950408889634


def _cparams(sem):
    return pltpu.CompilerParams(dimension_semantics=sem, vmem_limit_bytes=V7X_VMEM_LIMIT_BYTES)


def _const_spec(shape):
    nd = len(shape)
    return pl.BlockSpec(shape, lambda *_: (0,) * nd, pipeline_mode=pl.Buffered(1))


def _rms(x, g):
    return x * lax.rsqrt(jnp.mean(x * x, axis=-1, keepdims=True) + EPS) * g


def _sigmoid(x):
    return 1.0 / (1.0 + jnp.exp(-x))


def _dot(a, b):
    return jnp.dot(a, b, preferred_element_type=F32)


def _dot_nt(a, b):
    return lax.dot_general(a, b, (((1,), (1,)), ((), ())), preferred_element_type=F32)


def _dot_tn(a, b):
    return lax.dot_general(a, b, (((0,), (0,)), ((), ())), preferred_element_type=F32)


def _proj_body(x_ref, nw_ref, w_ref, cos_ref, s1_ref, s2_ref, qg_ref, kg_ref,
               hq_ref, zf_ref, hi_ref, hg_ref, aq_ref, ak_ref, av_ref, gh_ref, ga_ref):
    u = _rms(x_ref[...], nw_ref[...]).astype(BF16)

    def mm(i):
        return _dot(u, w_ref[:, _IN_OFF[i]:_IN_OFF[i + 1]])

    hq_ref[...] = mm(0).astype(BF16)
    zf_ref[0] = mm(1)
    zf_ref[1] = mm(2)
    hi_ref[...] = mm(3).astype(BF16)
    hg_ref[...] = mm(4)
    c, s1, s2 = cos_ref[...], s1_ref[...], s2_ref[...]

    def norm_rope(z, g, scale):
        y = _rms(z, g)
        y = y * c + pltpu.roll(y, ATT_HD - ROPE_HALF, 1) * s1 + pltpu.roll(y, ROPE_HALF, 1) * s2
        return y * scale

    zq = mm(5)
    qscale = (ATT_HD ** -0.5) * LOG2E
    for h in range(ATT_HEADS):
        hs = slice(h * ATT_HD, (h + 1) * ATT_HD)
        aq_ref[h] = norm_rope(zq[:, hs], qg_ref[...], qscale).T.astype(BF16)
    zk = mm(6)
    for h in range(ATT_KV):
        hs = slice(h * ATT_HD, (h + 1) * ATT_HD)
        ak_ref[:, hs] = norm_rope(zk[:, hs], kg_ref[...], 1.0).astype(BF16)
    zv = mm(7)
    for h in range(ATT_KV):
        av_ref[h] = zv[:, h * ATT_HD:(h + 1) * ATT_HD].T.astype(BF16)
    gh_ref[...] = mm(8)
    ga_ref[...] = mm(9)


def _proj(x, nw, w_in, cos, s1, s2, qg, kg):
    bsz, t, d = x.shape
    tm = TOKEN_TILE
    tok = lambda w: pl.BlockSpec((None, tm, w), lambda b, i: (b, i, 0))
    tab = pl.BlockSpec((tm, ATT_HD), lambda b, i: (i, 0))
    sds = lambda w, dt: jax.ShapeDtypeStruct((bsz, t, w), dt)
    head_t = lambda nh: pl.BlockSpec((None, nh, ATT_HD, tm), lambda b, i: (b, 0, 0, i))
    sds_t = lambda nh: jax.ShapeDtypeStruct((bsz, nh, ATT_HD, t), BF16)
    return pl.pallas_call(
        _proj_body,
        grid=(bsz, t // tm),
        in_specs=[tok(d), _const_spec((1, d)), _const_spec((d, IN_WIDTH)), tab, tab, tab,
                  _const_spec((1, ATT_HD)), _const_spec((1, ATT_HD))],
        out_specs=[tok(HG_W),
                   pl.BlockSpec((2, None, tm, HG_W), lambda b, i: (0, b, i, 0)),
                   tok(HG_W), tok(HG_W), head_t(ATT_HEADS), tok(ATT_KV_W), head_t(ATT_KV),
                   tok(D_MODEL), tok(D_MODEL)],
        out_shape=[sds(HG_W, BF16), jax.ShapeDtypeStruct((2, bsz, t, HG_W), F32),
                   sds(HG_W, BF16), sds(HG_W, F32), sds_t(ATT_HEADS), sds(ATT_KV_W, BF16),
                   sds_t(ATT_KV), sds(D_MODEL, F32), sds(D_MODEL, F32)],
        compiler_params=_cparams(("parallel", "parallel")),
        name="proj",
    )(x, nw, w_in, cos, s1, s2, qg, kg)


def _rope_tables(t):
    pos_t = jnp.arange(t, dtype=I32)
    pos = jnp.stack([pos_t // GRID_W, pos_t % GRID_W], axis=-1).astype(F32)
    inv = jnp.power(ROPE_THETA, -jnp.arange(0, 2 * ROPE_HALF, 2, dtype=F32) / (2 * ROPE_HALF))
    ang = pos[:, :, None] * inv
    cos, sin = jnp.cos(ang), jnp.sin(ang)
    zero = jnp.zeros_like(sin)
    c = jnp.concatenate([cos, cos], axis=-1).reshape(t, ATT_HD)
    s1 = jnp.concatenate([-sin, zero], axis=-1).reshape(t, ATT_HD)
    s2 = jnp.concatenate([zero, sin], axis=-1).reshape(t, ATT_HD)
    return c, s1, s2


def _split3(x):
    a = x.astype(BF16)
    r = x - a.astype(F32)
    b = r.astype(BF16)
    c = (r - b.astype(F32)).astype(BF16)
    return a, b, c


def _hgrn_chunks(qs, zs, vs, lbs, sts, rev):
    C, LS, SUB = HG_CHUNK, HG_LEAF, 8
    nb = C // LS
    heads = range(len(qs))
    r_i = lax.broadcasted_iota(I32, (C, C), 0)
    c_i = lax.broadcasted_iota(I32, (C, C), 1)
    before = (c_i >= r_i) if rev else (c_i <= r_i)
    tri = jnp.where(before, 1.0, 0.0).astype(BF16)

    front = []
    for h in heads:
        f = lbs[h] + (1.0 - lbs[h]) * _sigmoid(zs[h])
        kk = 1.0 - f
        g1, g2, g3 = _split3(jnp.log(f))
        b2 = (_dot(tri, g1) + _dot(tri, g2) + _dot(tri, g3)) * LOG2E
        front.append((qs[h].astype(F32), kk, b2, b2[0:1] if rev else b2[C - 1:C]))

    inter = []
    for h in heads:
        qf, kk, b2, btot = front[h]
        o = _dot_nt((qf * jnp.exp2(b2)).astype(BF16), sts[h].astype(BF16))
        kd = (kk * jnp.exp2(btot - b2)).astype(BF16)
        inter.append((o, sts[h] * jnp.exp2(btot) + _dot_tn(vs[h], kd)))

    cols = range(1, nb) if rev else range(nb - 1)
    zeros_l = jnp.zeros((LS, HG_D), BF16)
    zero_row = jnp.zeros((LS, HG_D * (nb - 1)), BF16)
    a_off = []
    for h in heads:
        qf, kk, b2, _ = front[h]
        lhs, rhs_rows = [], {}
        for p, j in enumerate(cols):
            rr = LS * j if rev else LS * (j + 1) - 1
            ref = b2[rr:rr + 1]
            rows = slice(0, LS * j) if rev else slice(LS * (j + 1), C)
            scaled = (qf[rows] * jnp.exp2(b2[rows] - ref)).astype(BF16)
            pad = jnp.zeros((C - scaled.shape[0], HG_D), BF16)
            lhs.append(jnp.concatenate([scaled, pad] if rev else [pad, scaled], axis=0))
            kj = (kk[LS * j:LS * (j + 1)] * jnp.exp2(ref - b2[LS * j:LS * (j + 1)])).astype(BF16)
            rhs_rows[j] = jnp.concatenate([kj if i == p else zeros_l for i in range(nb - 1)], axis=1)
        kmat = jnp.concatenate([rhs_rows.get(j, zero_row) for j in range(nb)], axis=0)
        a_off.append(_dot_nt(jnp.concatenate(lhs, axis=1), kmat))

    lane = lax.broadcasted_iota(I32, (SUB, C), 1)
    a_diag = []
    for h in heads:
        qf, kk, b2, _ = front[h]
        tiles = [jnp.zeros((SUB, C), F32) for _ in range(C // SUB)]
        for s in range(C):
            blk = s // LS
            first, last = (LS * blk // SUB, s // SUB) if rev else (s // SUB, LS * (blk + 1) // SUB - 1)
            brow, krow = b2[s:s + 1], kk[s:s + 1]
            for t in range(first, last + 1):
                rs = slice(SUB * t, SUB * (t + 1))
                d = b2[rs] - brow
                if t == s // SUB:
                    d = jnp.minimum(d, 0.0)
                col = jnp.sum(qf[rs] * jnp.exp2(d) * krow, axis=1, keepdims=True)
                tiles[t] = jnp.where(lane == s, col, tiles[t])
        a_diag.append(jnp.where((r_i <= c_i) if rev else (r_i >= c_i), jnp.concatenate(tiles, axis=0), 0.0))

    return [(inter[h][0] + _dot((a_off[h] + a_diag[h]).astype(BF16), vs[h]), inter[h][1]) for h in heads]


def _hgrn_body(q_ref, z_ref, v_ref, lb_ref, o_ref, st_ref):
    d = pl.program_id(1)
    j = pl.program_id(2)
    nch = HG_BLOCK // HG_CHUNK
    hss = [slice(h * HG_D, (h + 1) * HG_D) for h in range(HG_HEADS)]

    @pl.when(j == 0)
    def _():
        st_ref[...] = jnp.zeros_like(st_ref)

    def run(rev):
        def step(i, carry):
            c = (nch - 1 - i) if rev else i
            rows = pl.ds(pl.multiple_of(c * HG_CHUNK, HG_CHUNK), HG_CHUNK)
            outs = _hgrn_chunks([q_ref[rows, hs] for hs in hss], [z_ref[rows, hs] for hs in hss],
                                [v_ref[rows, hs] for hs in hss],
                                [lb_ref[int(rev):int(rev) + 1, hs] for hs in hss],
                                [st_ref[h] for h in range(HG_HEADS)], rev)
            for h, (o, st) in enumerate(outs):
                o_ref[rows, hss[h]] = o
                st_ref[h] = st
            return carry
        lax.fori_loop(0, nch, step, 0, unroll=HG_UNROLL)

    @pl.when(d == 0)
    def _():
        run(False)

    @pl.when(d == 1)
    def _():
        run(True)


def _hgrn(hq, zf, hi, lb):
    bsz, t, _ = hq.shape
    nblk = t // HG_BLOCK

    def blk(d, j):
        return jnp.where(d == 0, j, nblk - 1 - j)

    tok = pl.BlockSpec((None, HG_BLOCK, HG_W), lambda b, d, j: (b, blk(d, j), 0))
    dirtok = pl.BlockSpec((None, None, HG_BLOCK, HG_W), lambda b, d, j: (d, b, blk(d, j), 0))
    return pl.pallas_call(
        _hgrn_body,
        grid=(bsz, 2, nblk),
        in_specs=[tok, dirtok, tok, _const_spec((2, HG_W))],
        out_specs=dirtok,
        out_shape=jax.ShapeDtypeStruct((2, bsz, t, HG_W), F32),
        scratch_shapes=[pltpu.VMEM((HG_HEADS, HG_D, HG_D), F32)],
        compiler_params=_cparams(("parallel", "arbitrary", "arbitrary")),
        name="hgrn",
    )(hq, zf, hi, lb)


def _attn_body(bound_s, q_ref, k_ref, v_ref, o_ref, qt_ref, s0_ref, m_ref, l_ref, acc_ref, *, unroll, bounded):
    tq, tks = ATT_TQ, ATT_TKS
    nsub = k_ref.shape[0] // tks
    for g in range(ATT_GROUP):
        qt_ref[:, g * tq:(g + 1) * tq] = q_ref[g]
    m_ref[...] = jnp.full_like(m_ref, -jnp.inf)
    l_ref[...] = jnp.zeros_like(l_ref)
    acc_ref[...] = jnp.zeros_like(acc_ref)
    qt = qt_ref[...]
    s0_ref[...] = _dot(k_ref[0:tks, :], qt)
    bound = bound_s[0]

    def group(i, carry):
        m, l, acc = m_ref[...], l_ref[...], acc_ref[...]
        s_next = s0_ref[...]
        for u in range(unroll):
            c = i * unroll + u
            s = s_next
            nxt = jnp.minimum(c + 1, nsub - 1)
            s_next = _dot(k_ref[pl.ds(pl.multiple_of(nxt * tks, tks), tks), :], qt)
            vt = v_ref[:, pl.ds(pl.multiple_of(c * tks, tks), tks)]
            if bounded:
                p = jnp.exp2(s - bound)
                l = l + jnp.sum(p, axis=0, keepdims=True)
                acc = acc + _dot(vt, p.astype(BF16))
            else:
                m_new = jnp.maximum(m, jnp.max(s, axis=0, keepdims=True))
                alpha = jnp.exp2(m - m_new)
                p = jnp.exp2(s - m_new)
                l = alpha * l + jnp.sum(p, axis=0, keepdims=True)
                acc = alpha * acc + _dot(vt, p.astype(BF16))
                m = m_new
        s0_ref[...] = s_next
        m_ref[...], l_ref[...], acc_ref[...] = m, l, acc
        return carry

    lax.fori_loop(0, nsub // unroll, group, 0)
    out = acc_ref[...] / l_ref[...]
    for g in range(ATT_GROUP):
        o_ref[:, g * ATT_HD:(g + 1) * ATT_HD] = out[:, g * tq:(g + 1) * tq].T.astype(BF16)


def _attn_call(bound, aq_t, ak, av_t, bounded):
    bsz, _, _, t = aq_t.shape
    tq = ATT_TQ
    nsub = t // ATT_TKS
    unroll = ATT_UNROLL if nsub % ATT_UNROLL == 0 else 1
    gw = ATT_GROUP * ATT_HD
    return pl.pallas_call(
        functools.partial(_attn_body, unroll=unroll, bounded=bounded),
        grid_spec=pltpu.PrefetchScalarGridSpec(
            num_scalar_prefetch=1,
            grid=(bsz, ATT_KV, t // tq),
            in_specs=[pl.BlockSpec((None, ATT_GROUP, ATT_HD, tq), lambda b, h, qi, *_: (b, h, 0, qi)),
                      pl.BlockSpec((None, t, ATT_HD), lambda b, h, qi, *_: (b, 0, h)),
                      pl.BlockSpec((None, None, ATT_HD, t), lambda b, h, qi, *_: (b, h, 0, 0))],
            out_specs=pl.BlockSpec((None, tq, gw), lambda b, h, qi, *_: (b, qi, h)),
            scratch_shapes=[pltpu.VMEM((ATT_HD, ATT_GROUP * tq), BF16),
                            pltpu.VMEM((ATT_TKS, ATT_GROUP * tq), F32),
                            pltpu.VMEM((1, ATT_GROUP * tq), F32),
                            pltpu.VMEM((1, ATT_GROUP * tq), F32),
                            pltpu.VMEM((ATT_HD, ATT_GROUP * tq), F32)]),
        out_shape=jax.ShapeDtypeStruct((bsz, t, ATT_W), BF16),
        compiler_params=_cparams(("parallel", "parallel", "arbitrary")),
        name="attn_bounded" if bounded else "attn",
    )(bound, aq_t, ak, av_t)


def _attn(bound, aq_t, ak, av_t):
    return lax.cond(bound[0] <= ATT_MAX_BOUNDED_SCORE,
                    functools.partial(_attn_call, bounded=True),
                    functools.partial(_attn_call, bounded=False),
                    bound, aq_t, ak, av_t)


def _score_bound(q_gain, k_gain):
    qscale = (ATT_HD ** -0.5) * LOG2E
    b = 1.01 * ATT_HD * qscale * jnp.max(jnp.abs(q_gain)) * jnp.max(jnp.abs(k_gain))
    return b.reshape(1).astype(F32)


def _memkv_body(m_ref, nw_ref, w_ref, o_ref):
    o_ref[...] = _dot(_rms(m_ref[...], nw_ref[...]).astype(BF16), w_ref[...]).astype(BF16)


def _memkv(mem, nw, w_ckv):
    bsz, m, d = mem.shape
    return pl.pallas_call(
        _memkv_body,
        grid=(bsz,),
        in_specs=[pl.BlockSpec((None, m, d), lambda b: (b, 0, 0)), _const_spec((1, d)),
                  _const_spec((d, 2 * X_W))],
        out_specs=pl.BlockSpec((None, m, 2 * X_W), lambda b: (b, 0, 0)),
        out_shape=jax.ShapeDtypeStruct((bsz, m, 2 * X_W), BF16),
        compiler_params=_cparams(("parallel",)),
        name="memkv",
    )(mem, nw, w_ckv)


def _post_body(x_ref, o2_ref, hg_ref, oa_ref, gh_ref, ga_ref, kv_ref,
               hgn_ref, wbh_ref, wba_ref, wo_ref, nx_ref, wcq_ref, wco_ref, nf_ref, wr_ref,
               h2_ref, u3_ref, aff_ref):
    o = o2_ref[0] + o2_ref[1]
    parts = []
    for h in range(HG_HEADS):
        oh = o[:, h * HG_D:(h + 1) * HG_D]
        parts.append(oh * lax.rsqrt(jnp.mean(oh * oh, axis=-1, keepdims=True) + EPS))
    hgv = hg_ref[...]
    oh = (jnp.concatenate(parts, axis=1) * hgn_ref[...] * (hgv * _sigmoid(hgv))).astype(BF16)
    merged = (_sigmoid(gh_ref[...]) * _dot(oh, wbh_ref[...])
              + _sigmoid(ga_ref[...]) * _dot(oa_ref[...], wba_ref[...]))
    h1 = x_ref[...] + _dot(merged.astype(BF16), wo_ref[...])

    u2 = _rms(h1, nx_ref[...]).astype(BF16)
    qx = (_dot(u2, wcq_ref[...]) * (X_HD ** -0.5)).astype(BF16)
    outs = []
    for h in range(X_HEADS):
        hs = slice(h * X_HD, (h + 1) * X_HD)
        s = _dot_nt(qx[:, hs], kv_ref[:, hs])
        p = jnp.exp(s - jnp.max(s, axis=1, keepdims=True))
        ov = _dot(p.astype(BF16), kv_ref[:, X_W + h * X_HD:X_W + (h + 1) * X_HD])
        outs.append(ov / jnp.sum(p, axis=1, keepdims=True))
    h2 = h1 + _dot(jnp.concatenate(outs, axis=1).astype(BF16), wco_ref[...])
    h2_ref[...] = h2

    u3 = _rms(h2, nf_ref[...]).astype(BF16)
    u3_ref[...] = u3
    logits = _dot_nt(wr_ref[...], u3)
    e = jnp.exp(logits - jnp.max(logits, axis=0, keepdims=True))
    aff_ref[...] = e / jnp.sum(e, axis=0, keepdims=True)


def _post(x, o2, hg, oa, gh, ga, kv, hgn, wbh, wba, wo, nx, wcq, wco, nf, wr_t):
    bsz, t, d = x.shape
    tm = POST_TILE
    nt = t // tm
    m = kv.shape[1]
    tok = lambda w: pl.BlockSpec((None, tm, w), lambda b, i: (b, i, 0))
    return pl.pallas_call(
        _post_body,
        grid=(bsz, nt),
        in_specs=[tok(d), pl.BlockSpec((2, None, tm, HG_W), lambda b, i: (0, b, i, 0)), tok(HG_W),
                  tok(ATT_W), tok(d), tok(d), pl.BlockSpec((None, m, 2 * X_W), lambda b, i: (b, 0, 0)),
                  _const_spec((1, HG_W)), _const_spec((HG_W, d)), _const_spec((ATT_W, d)),
                  _const_spec((d, d)), _const_spec((1, d)), _const_spec((d, X_W)), _const_spec((X_W, d)),
                  _const_spec((1, d)), _const_spec((N_EXPERTS, d))],
        out_specs=[tok(d), tok(d), pl.BlockSpec((N_EXPERTS, tm), lambda b, i: (0, b * nt + i))],
        out_shape=[jax.ShapeDtypeStruct((bsz, t, d), F32), jax.ShapeDtypeStruct((bsz, t, d), BF16),
                   jax.ShapeDtypeStruct((N_EXPERTS, bsz * t), F32)],
        compiler_params=_cparams(("parallel", "parallel")),
        name="post",
    )(x, o2, hg, oa, gh, ga, kv, hgn, wbh, wba, wo, nx, wcq, wco, nf, wr_t)


def _route_body(aff_ref, pos_ref, off_ref, cnt_ref, *, cap):
    n = aff_ref.shape[1]
    nblk = n // ROUTE_BLOCK
    bits = pltpu.bitcast(aff_ref[...], I32)

    def count_ge(th):
        return jnp.sum(jnp.where(bits >= th, 1.0, 0.0), axis=1, keepdims=True)

    def bisect(_, lohi):
        lo, hi = lohi
        mid = lo + lax.shift_right_logical(hi - lo, 1)
        ok = count_ge(mid) >= cap
        return jnp.where(ok, mid, lo), jnp.where(ok, hi, mid)

    lo0 = jnp.zeros((N_EXPERTS, 1), I32)
    hi0 = jnp.full((N_EXPERTS, 1), 0x7F800000, I32)
    thr, _ = lax.fori_loop(0, 31, bisect, (lo0, hi0))
    n_gt = jnp.sum(jnp.where(bits > thr, 1.0, 0.0), axis=1, keepdims=True)
    need = cap - n_gt

    r_i = lax.broadcasted_iota(I32, (ROUTE_BLOCK, ROUTE_BLOCK), 0)
    c_i = lax.broadcasted_iota(I32, (ROUTE_BLOCK, ROUTE_BLOCK), 1)
    tri = jnp.where(r_i < c_i, 1.0, 0.0).astype(BF16)
    lane = lax.broadcasted_iota(I32, (N_EXPERTS, MAX_ROUTE_BLOCKS), 1)

    off_ref[...] = jnp.zeros_like(off_ref)
    cnt_ref[...] = jnp.zeros_like(cnt_ref)

    def scan(j, carry):
        run_sel, run_eq = carry
        cols = pl.ds(pl.multiple_of(j * ROUTE_BLOCK, ROUTE_BLOCK), ROUTE_BLOCK)
        b = pltpu.bitcast(aff_ref[:, cols], I32)
        eq = jnp.where(b == thr, 1.0, 0.0)
        eq_rank = run_eq + _dot(eq.astype(BF16), tri)
        sel = jnp.where(b > thr, 1.0, jnp.where(eq_rank < need, eq, 0.0))
        pos = run_sel + _dot(sel.astype(BF16), tri)
        pos_ref[:, cols] = jnp.where(sel > 0.0, pos, -1.0).astype(I32)
        c = jnp.sum(sel, axis=1, keepdims=True)
        off_ref[...] = jnp.where(lane == j, run_sel.astype(I32), off_ref[...])
        cnt_ref[...] = jnp.where(lane == j, c.astype(I32), cnt_ref[...])
        return run_sel + c, run_eq + jnp.sum(eq, axis=1, keepdims=True)

    z1 = n_gt * 0.0
    lax.fori_loop(0, nblk, scan, (z1, z1))


def _route(aff_t, cap):
    n = aff_t.shape[1]
    assert n % ROUTE_BLOCK == 0 and n // ROUTE_BLOCK <= MAX_ROUTE_BLOCKS
    tbl = jax.ShapeDtypeStruct((N_EXPERTS, MAX_ROUTE_BLOCKS), I32)
    return pl.pallas_call(
        functools.partial(_route_body, cap=cap),
        out_shape=[jax.ShapeDtypeStruct((N_EXPERTS, n), I32), tbl, tbl],
        compiler_params=pltpu.CompilerParams(vmem_limit_bytes=V7X_VMEM_LIMIT_BYTES),
        name="route",
    )(aff_t)


def _dispatch_body(off_s, cnt_s, pos_ref, aff_ref, u_ref, xe_ref, stage_ref, xstage_ref, tail_ref, sem_ref,
                   xsem_ref, *, cap_pad):
    j = pl.program_id(0)
    last = pl.num_programs(0) - 1
    slot = j % 2

    @pl.when(j == 0)
    def _():
        tail_ref[...] = jnp.zeros_like(tail_ref)

    def geom(e, jj):
        off = off_s[e * MAX_ROUTE_BLOCKS + jj]
        c = cnt_s[e * MAX_ROUTE_BLOCKS + jj]
        a = off & (ROW_ALIGN - 1)
        return off - a, a + c, jnp.where(c > 0, (a + c + ROW_CHUNK - 1) // ROW_CHUNK, 0)

    def dst(e, row):
        return xe_ref.at[pl.ds(pl.multiple_of(e * cap_pad + row, ROW_ALIGN), ROW_CHUNK), :]

    def first_chunks(jj, sl):
        return [pltpu.make_async_copy(stage_ref.at[sl, pl.ds(e * ROW_CHUNK, ROW_CHUNK), :],
                                      dst(e, geom(e, jj)[0]), sem_ref.at[sl]) for e in range(N_EXPERTS)]

    terms = [t.astype(F32) for t in _split3(aff_ref[...].T)]
    fill = jnp.zeros((ROUTE_BLOCK, GATE_LANES - GATE_TERMS * N_EXPERTS), F32)
    u = jnp.concatenate([u_ref[...], jnp.concatenate(terms + [fill], axis=1).astype(BF16)], axis=1)
    r_i = lax.broadcasted_iota(I32, (ROW_CHUNK, ROUTE_BLOCK), 0)

    def onehot(e, row0):
        return jnp.where(r_i == pos_ref[pl.ds(e, 1), :] - row0, 1.0, 0.0).astype(BF16)

    rows = _dot(jnp.concatenate([onehot(e, geom(e, j)[0]) for e in range(N_EXPERTS)], axis=0), u)
    for e in range(N_EXPERTS):
        r0 = e * ROW_CHUNK
        stage_ref[slot, r0:r0 + ROW_ALIGN, :] = (rows[r0:r0 + ROW_ALIGN] + tail_ref[e].astype(F32)).astype(BF16)
        stage_ref[slot, r0 + ROW_ALIGN:r0 + ROW_CHUNK, :] = rows[r0 + ROW_ALIGN:r0 + ROW_CHUNK].astype(BF16)

    @pl.when(j > 0)
    def _():
        for cp in first_chunks(j - 1, 1 - slot):
            cp.wait()

    for cp in first_chunks(j, slot):
        cp.start()

    def expert(e, carry):
        base, end, nch = geom(e, j)

        def extra(ch, carry):
            row0 = base + ch * ROW_CHUNK
            xstage_ref[...] = _dot(onehot(e, row0), u).astype(BF16)
            cp = pltpu.make_async_copy(xstage_ref, dst(e, row0), xsem_ref)
            cp.start()
            cp.wait()
            return carry

        lax.fori_loop(1, jnp.maximum(nch, 1), extra, carry)

        @pl.when(nch > 0)
        def _():
            part = end & (ROW_ALIGN - 1)
            r0 = pl.multiple_of((end - part) & (ROW_CHUNK - 1), ROW_ALIGN)
            keep = jnp.where(nch > 1, xstage_ref[pl.ds(r0, ROW_ALIGN), :],
                             stage_ref[slot, pl.ds(pl.multiple_of(e * ROW_CHUNK + r0, ROW_ALIGN), ROW_ALIGN), :])
            tail_ref[e] = jnp.where(part > 0, keep, jnp.zeros_like(keep))

        return carry

    lax.fori_loop(0, N_EXPERTS, expert, 0)

    @pl.when(j == last)
    def _():
        for cp in first_chunks(j, slot):
            cp.wait()
        xstage_ref[...] = jnp.zeros_like(xstage_ref)
        pads = [pltpu.make_async_copy(xstage_ref, dst(e, cap_pad - ROW_CHUNK), xsem_ref) for e in range(N_EXPERTS)]
        for cp in pads:
            cp.start()
        for cp in pads:
            cp.wait()


def _dispatch(off, cnt, pos_t, aff_t, u, cap):
    n, d = u.shape
    d_aug = d + GATE_LANES
    cap_pad = cap + ROW_CHUNK
    nblk = n // ROUTE_BLOCK
    eblk = pl.BlockSpec((N_EXPERTS, ROUTE_BLOCK), lambda j, *_: (0, j))
    return pl.pallas_call(
        functools.partial(_dispatch_body, cap_pad=cap_pad),
        grid_spec=pltpu.PrefetchScalarGridSpec(
            num_scalar_prefetch=2,
            grid=(nblk,),
            in_specs=[eblk, eblk, pl.BlockSpec((ROUTE_BLOCK, d), lambda j, *_: (j, 0))],
            out_specs=pl.BlockSpec(memory_space=pl.ANY),
            scratch_shapes=[pltpu.VMEM((2, N_EXPERTS * ROW_CHUNK, d_aug), BF16),
                            pltpu.VMEM((ROW_CHUNK, d_aug), BF16),
                            pltpu.VMEM((N_EXPERTS, ROW_ALIGN, d_aug), BF16),
                            pltpu.SemaphoreType.DMA((2,)),
                            pltpu.SemaphoreType.DMA(())]),
        out_shape=jax.ShapeDtypeStruct((N_EXPERTS * cap_pad, d_aug), BF16),
        compiler_params=_cparams(("arbitrary",)),
        name="dispatch",
    )(off, cnt, pos_t, aff_t, u)


_FFN_COLS = 512


def _ffn_body(x_ref, wg_ref, wu_ref, wd_ref, o_ref):
    x = x_ref[:, :D_MODEL]
    acc = None
    for c in range(D_EXPERT // _FFN_COLS):
        cs = slice(c * _FFN_COLS, (c + 1) * _FFN_COLS)
        g = _dot(x, wg_ref[:, cs])
        hid = (g * _sigmoid(g) * _dot(x, wu_ref[:, cs])).astype(BF16)
        part = _dot(hid, wd_ref[cs, :])
        acc = part if acc is None else acc + part
    lane = lax.broadcasted_iota(I32, (1, GATE_LANES), 1)
    mine = (lane < GATE_TERMS * N_EXPERTS) & (lane % N_EXPERTS == pl.program_id(0))
    gate = jnp.sum(jnp.where(mine, x_ref[:, D_MODEL:].astype(F32), 0.0), axis=1, keepdims=True)
    o_ref[...] = (acc * gate).astype(BF16)


def _ffn(xe, wg, wu, wd, cap):
    e, _, d_aug = xe.shape
    d = d_aug - GATE_LANES
    tm = 512 if cap % 512 == 0 else ROW_CHUNK
    wspec = lambda a, b: pl.BlockSpec((None, a, b), lambda ei, i: (ei, 0, 0))
    return pl.pallas_call(
        _ffn_body,
        grid=(e, cap // tm),
        in_specs=[pl.BlockSpec((None, tm, d_aug), lambda ei, i: (ei, i, 0)),
                  wspec(d, D_EXPERT), wspec(d, D_EXPERT), wspec(D_EXPERT, d)],
        out_specs=pl.BlockSpec((None, tm, d), lambda ei, i: (ei, i, 0)),
        out_shape=jax.ShapeDtypeStruct((e, cap, d), BF16),
        compiler_params=_cparams(("parallel", "parallel")),
        name="ffn",
    )(xe, wg, wu, wd)


def _combine_body(off_s, cnt_s, pos_ref, h_ref, nw_ref, ye_ref, y_ref, rows_ref, lhs_ref, xbuf_ref,
                  sem_ref, xsem_ref, acc_ref, *, cap):
    j = pl.program_id(0)
    nblk = pl.num_programs(0)
    per_e = ROW_CHUNK

    def geom(e, jj):
        off = off_s[e * MAX_ROUTE_BLOCKS + jj]
        c = cnt_s[e * MAX_ROUTE_BLOCKS + jj]
        a = off & (ROW_ALIGN - 1)
        return off - a, jnp.where(c > 0, (a + c + ROW_CHUNK - 1) // ROW_CHUNK, 0)

    def window(base, ch):
        lo = base + ch * ROW_CHUNK
        return lo, pl.multiple_of(jnp.minimum(lo, cap - ROW_CHUNK), ROW_ALIGN)

    def src(e, start):
        return ye_ref.at[pl.ds(pl.multiple_of(e * cap + start, ROW_ALIGN), ROW_CHUNK), :]

    def first_chunks(jj, slot):
        cps = []
        for e in range(N_EXPERTS):
            base, _ = geom(e, jj)
            _, start = window(base, 0)
            cps.append(pltpu.make_async_copy(
                src(e, start), rows_ref.at[slot, pl.ds(e * per_e, ROW_CHUNK), :], sem_ref.at[slot]))
        return cps

    @pl.when(j == 0)
    def _():
        for cp in first_chunks(0, 0):
            cp.start()

    @pl.when(j + 1 < nblk)
    def _():
        for cp in first_chunks(j + 1, (j + 1) % 2):
            cp.start()

    pos_c = pos_ref[...].T
    lane = lax.broadcasted_iota(I32, (ROUTE_BLOCK, ROW_CHUNK), 1)
    for e in range(N_EXPERTS):
        base, _ = geom(e, j)
        lo, start = window(base, 0)
        r = pos_c[:, e:e + 1] - start
        hit = lane == jnp.where(r >= lo - start, r, -1)
        lhs_ref[:, e * per_e:(e + 1) * per_e] = jnp.where(hit, 1.0, 0.0).astype(BF16)

    slot = j % 2
    for cp in first_chunks(j, slot):
        cp.wait()
    acc_ref[...] = _dot(lhs_ref[...], rows_ref[slot])

    r_i = lax.broadcasted_iota(I32, (ROW_CHUNK, ROUTE_BLOCK), 0)

    def expert(e, carry):
        base, nch = geom(e, j)

        def extra(ch, carry):
            lo, start = window(base, ch)
            cp = pltpu.make_async_copy(src(e, start), xbuf_ref, xsem_ref)
            cp.start()
            cp.wait()
            pos = pos_ref[pl.ds(e, 1), :]
            hit = (r_i == pos - start) & (pos >= lo)
            acc_ref[...] += _dot_tn(jnp.where(hit, 1.0, 0.0).astype(BF16), xbuf_ref[...])
            return carry

        return lax.fori_loop(1, jnp.maximum(nch, 1), extra, carry)

    lax.fori_loop(0, N_EXPERTS, expert, 0)
    y_ref[...] = _rms(h_ref[...] + acc_ref[...], nw_ref[...])


def _combine(off, cnt, pos_t, h2, nw, ye, cap):
    n, d = h2.shape
    nblk = n // ROUTE_BLOCK
    eblk = pl.BlockSpec((N_EXPERTS, ROUTE_BLOCK), lambda j, *_: (0, j))
    tok = pl.BlockSpec((ROUTE_BLOCK, d), lambda j, *_: (j, 0))
    kdim = N_EXPERTS * ROW_CHUNK
    return pl.pallas_call(
        functools.partial(_combine_body, cap=cap),
        grid_spec=pltpu.PrefetchScalarGridSpec(
            num_scalar_prefetch=2,
            grid=(nblk,),
            in_specs=[eblk, tok, pl.BlockSpec((1, d), lambda j, *_: (0, 0)),
                      pl.BlockSpec(memory_space=pl.ANY)],
            out_specs=tok,
            scratch_shapes=[pltpu.VMEM((2, kdim, d), BF16),
                            pltpu.VMEM((ROUTE_BLOCK, kdim), BF16),
                            pltpu.VMEM((ROW_CHUNK, d), BF16),
                            pltpu.SemaphoreType.DMA((2,)),
                            pltpu.SemaphoreType.DMA(()),
                            pltpu.VMEM((ROUTE_BLOCK, d), F32)]),
        out_shape=jax.ShapeDtypeStruct((n, d), F32),
        compiler_params=_cparams(("arbitrary",)),
        name="combine",
    )(off, cnt, pos_t, h2, nw, ye)


def _trunk(x, mem, lb, p):
    bsz, t, d = x.shape
    n = bsz * t
    cap = max(1, EC_FACTOR * n // N_EXPERTS)
    assert cap % ROW_CHUNK == 0 and t % HG_BLOCK == 0 and t % ATT_TKS == 0
    cos, s1, s2 = _rope_tables(t)
    hq, zf, hi, hg, aq, ak, av, gh, ga = _proj(x, p["norm_mix"], p["w_in"], cos, s1, s2, p["q_norm"], p["k_norm"])
    o2 = _hgrn(hq, zf, hi, lb)
    oa = _attn(p["score_bound"], aq, ak, av)
    kv = _memkv(mem, p["norm_mem"], p["w_ckv"])
    h2, u3, aff_t = _post(x, o2, hg, oa, gh, ga, kv, p["hgrn_norm"], p["w_br_hgrn"], p["w_br_attn"], p["w_out"],
                          p["norm_x"], p["w_cq"], p["w_co"], p["norm_ffn"], p["w_router_t"])
    pos_t, off, cnt = _route(aff_t, cap)
    off, cnt = off.reshape(-1), cnt.reshape(-1)
    xe = _dispatch(off, cnt, pos_t, aff_t, u3.reshape(n, d), cap)
    ye = _ffn(xe.reshape(N_EXPERTS, cap + ROW_CHUNK, d + GATE_LANES), p["w_gate"], p["w_up"], p["w_down"], cap)
    y = _combine(off, cnt, pos_t, h2.reshape(n, d), p["norm_final"], ye.reshape(N_EXPERTS * cap, d), cap)
    return y.reshape(bsz, t, d)


def kernel(x_prompt, x_sample, mem_prompt, mem_sample, norm_mix, w_in, lb_logits, hgrn_norm, q_norm, k_norm,
           w_br_hgrn, w_br_attn, w_out, norm_x, norm_mem, w_cq, w_ckv, w_co, norm_ffn, w_router, w_gate, w_up,
           w_down, norm_final):
    row = lambda v: v.reshape(1, -1).astype(F32)
    p = {
        "norm_mix": row(norm_mix[0]), "w_in": w_in[0].astype(BF16),
        "hgrn_norm": row(hgrn_norm[0]), "q_norm": row(q_norm[0]), "k_norm": row(k_norm[0]),
        "w_br_hgrn": w_br_hgrn[0].astype(BF16), "w_br_attn": w_br_attn[0].astype(BF16),
        "w_out": w_out[0].astype(BF16), "norm_x": row(norm_x[0]), "norm_mem": row(norm_mem[0]),
        "w_cq": w_cq[0].astype(BF16), "w_ckv": w_ckv[0].astype(BF16), "w_co": w_co[0].astype(BF16),
        "norm_ffn": row(norm_ffn[0]), "w_router_t": w_router[0].T.astype(BF16),
        "w_gate": w_gate[0].astype(BF16), "w_up": w_up[0].astype(BF16), "w_down": w_down[0].astype(BF16),
        "norm_final": row(norm_final), "score_bound": _score_bound(q_norm[0], k_norm[0]),
    }
    lb = jnp.cumsum(jax.nn.softmax(lb_logits.astype(F32), axis=1), axis=1)[:, 0]
    return (_trunk(x_prompt, mem_prompt, lb, p), _trunk(x_sample, mem_sample, lb, p))
```

```python
import functools

import jax
import jax.numpy as jnp
from jax import lax
from jax.experimental import pallas as pl
from jax.experimental.pallas import tpu as pltpu

F32 = jnp.float32
BF16 = jnp.bfloat16
I32 = jnp.int32

D_MODEL = 1024
EPS = 1e-6
HG_HEADS = 4
HG_D = 128
HG_W = HG_HEADS * HG_D
HG_CHUNK = 64
HG_LEAF = 16
HG_BLOCK = 512
HG_UNROLL = 4
ATT_HEADS = 8
ATT_KV = 2
ATT_GROUP = ATT_HEADS // ATT_KV
ATT_HD = 128
ATT_W = ATT_HEADS * ATT_HD
ATT_KV_W = ATT_KV * ATT_HD
GRID_W = 64
ROPE_HALF = ATT_HD // 4
ROPE_THETA = 10000.0
ATT_TQ = 256
ATT_TKS = 512
ATT_UNROLL = 4
ATT_TKS_BOUNDED = 256
ATT_UNROLL_BOUNDED = 16
ATT_MAX_BOUNDED_SCORE = 60.0
X_HEADS = 4
X_HD = 128
X_W = X_HEADS * X_HD
N_EXPERTS = 16
EC_FACTOR = 2
D_EXPERT = 2048
ROUTE_BLOCK = 512
ROW_CHUNK = 128
ROW_ALIGN = 16
MAX_ROUTE_BLOCKS = 128
GATE_TERMS = 3
GATE_LANES = 128
_IN_SIZES = (HG_W, HG_W, HG_W, HG_W, HG_W, ATT_W, ATT_KV_W, ATT_KV_W, D_MODEL, D_MODEL)
_IN_OFF = tuple(sum(_IN_SIZES[:i]) for i in range(len(_IN_SIZES) + 1))
IN_WIDTH = _IN_OFF[-1]

TOKEN_TILE = 256
POST_TILE = 512
V7X_VMEM_LIMIT_BYTES = 56 * 1024 * 1024
LOG2E = 1.4426950408889634


def _cparams(sem):
    return pltpu.CompilerParams(dimension_semantics=sem, vmem_limit_bytes=V7X_VMEM_LIMIT_BYTES)


def _const_spec(shape):
    nd = len(shape)
    return pl.BlockSpec(shape, lambda *_: (0,) * nd, pipeline_mode=pl.Buffered(1))


def _rms(x, g):
    return x * lax.rsqrt(jnp.mean(x * x, axis=-1, keepdims=True) + EPS) * g


def _sigmoid(x):
    return 1.0 / (1.0 + jnp.exp(-x))


def _dot(a, b):
    return jnp.dot(a, b, preferred_element_type=F32)


def _dot_nt(a, b):
    return lax.dot_general(a, b, (((1,), (1,)), ((), ())), preferred_element_type=F32)


def _dot_tn(a, b):
    return lax.dot_general(a, b, (((0,), (0,)), ((), ())), preferred_element_type=F32)


def _proj_body(x_ref, nw_ref, w_ref, cos_ref, s1_ref, s2_ref, qg_ref, kg_ref,
               hq_ref, zf_ref, hi_ref, hg_ref, aq_ref, ak_ref, av_ref, gh_ref, ga_ref):
    u = _rms(x_ref[...], nw_ref[...]).astype(BF16)

    def mm(i):
        return _dot(u, w_ref[:, _IN_OFF[i]:_IN_OFF[i + 1]])

    hq_ref[...] = mm(0).astype(BF16)
    zf_ref[0] = mm(1)
    zf_ref[1] = mm(2)
    hi_ref[...] = mm(3).astype(BF16)
    hg_ref[...] = mm(4)
    c, s1, s2 = cos_ref[...], s1_ref[...], s2_ref[...]

    def norm_rope(z, g, scale):
        y = _rms(z, g)
        y = y * c + pltpu.roll(y, ATT_HD - ROPE_HALF, 1) * s1 + pltpu.roll(y, ROPE_HALF, 1) * s2
        return y * scale

    zq = mm(5)
    qscale = (ATT_HD ** -0.5) * LOG2E
    for h in range(ATT_HEADS):
        hs = slice(h * ATT_HD, (h + 1) * ATT_HD)
        aq_ref[h] = norm_rope(zq[:, hs], qg_ref[...], qscale).T.astype(BF16)
    zk = mm(6)
    for h in range(ATT_KV):
        hs = slice(h * ATT_HD, (h + 1) * ATT_HD)
        ak_ref[:, hs] = norm_rope(zk[:, hs], kg_ref[...], 1.0).astype(BF16)
    zv = mm(7)
    for h in range(ATT_KV):
        av_ref[h] = zv[:, h * ATT_HD:(h + 1) * ATT_HD].T.astype(BF16)
    gh_ref[...] = mm(8)
    ga_ref[...] = mm(9)


def _proj(x, nw, w_in, cos, s1, s2, qg, kg):
    bsz, t, d = x.shape
    tm = TOKEN_TILE
    tok = lambda w: pl.BlockSpec((None, tm, w), lambda b, i: (b, i, 0))
    tab = pl.BlockSpec((tm, ATT_HD), lambda b, i: (i, 0))
    sds = lambda w, dt: jax.ShapeDtypeStruct((bsz, t, w), dt)
    head_t = lambda nh: pl.BlockSpec((None, nh, ATT_HD, tm), lambda b, i: (b, 0, 0, i))
    sds_t = lambda nh: jax.ShapeDtypeStruct((bsz, nh, ATT_HD, t), BF16)
    return pl.pallas_call(
        _proj_body,
        grid=(bsz, t // tm),
        in_specs=[tok(d), _const_spec((1, d)), _const_spec((d, IN_WIDTH)), tab, tab, tab,
                  _const_spec((1, ATT_HD)), _const_spec((1, ATT_HD))],
        out_specs=[tok(HG_W),
                   pl.BlockSpec((2, None, tm, HG_W), lambda b, i: (0, b, i, 0)),
                   tok(HG_W), tok(HG_W), head_t(ATT_HEADS), tok(ATT_KV_W), head_t(ATT_KV),
                   tok(D_MODEL), tok(D_MODEL)],
        out_shape=[sds(HG_W, BF16), jax.ShapeDtypeStruct((2, bsz, t, HG_W), F32),
                   sds(HG_W, BF16), sds(HG_W, F32), sds_t(ATT_HEADS), sds(ATT_KV_W, BF16),
                   sds_t(ATT_KV), sds(D_MODEL, F32), sds(D_MODEL, F32)],
        compiler_params=_cparams(("parallel", "parallel")),
        name="proj",
    )(x, nw, w_in, cos, s1, s2, qg, kg)


def _rope_tables(t):
    pos_t = jnp.arange(t, dtype=I32)
    pos = jnp.stack([pos_t // GRID_W, pos_t % GRID_W], axis=-1).astype(F32)
    inv = jnp.power(ROPE_THETA, -jnp.arange(0, 2 * ROPE_HALF, 2, dtype=F32) / (2 * ROPE_HALF))
    ang = pos[:, :, None] * inv
    cos, sin = jnp.cos(ang), jnp.sin(ang)
    zero = jnp.zeros_like(sin)
    c = jnp.concatenate([cos, cos], axis=-1).reshape(t, ATT_HD)
    s1 = jnp.concatenate([-sin, zero], axis=-1).reshape(t, ATT_HD)
    s2 = jnp.concatenate([zero, sin], axis=-1).reshape(t, ATT_HD)
    return c, s1, s2


def _split3(x):
    a = x.astype(BF16)
    r = x - a.astype(F32)
    b = r.astype(BF16)
    c = (r - b.astype(F32)).astype(BF16)
    return a, b, c


def _hgrn_chunks(qs, zs, vs, lbs, sts, rev):
    C, LS, SUB = HG_CHUNK, HG_LEAF, 8
    nb = C // LS
    heads = range(len(qs))
    r_i = lax.broadcasted_iota(I32, (C, C), 0)
    c_i = lax.broadcasted_iota(I32, (C, C), 1)
    before = (c_i >= r_i) if rev else (c_i <= r_i)
    tri = jnp.where(before, 1.0, 0.0).astype(BF16)

    front = []
    for h in heads:
        f = lbs[h] + (1.0 - lbs[h]) * _sigmoid(zs[h])
        kk = 1.0 - f
        g1, g2, g3 = _split3(jnp.log(f))
        b2 = (_dot(tri, g1) + _dot(tri, g2) + _dot(tri, g3)) * LOG2E
        front.append((qs[h].astype(F32), kk, b2, b2[0:1] if rev else b2[C - 1:C]))

    inter = []
    for h in heads:
        qf, kk, b2, btot = front[h]
        o = _dot_nt((qf * jnp.exp2(b2)).astype(BF16), sts[h].astype(BF16))
        kd = (kk * jnp.exp2(btot - b2)).astype(BF16)
        inter.append((o, sts[h] * jnp.exp2(btot) + _dot_tn(vs[h], kd)))

    cols = range(1, nb) if rev else range(nb - 1)
    zeros_l = jnp.zeros((LS, HG_D), BF16)
    zero_row = jnp.zeros((LS, HG_D * (nb - 1)), BF16)
    a_off = []
    for h in heads:
        qf, kk, b2, _ = front[h]
        lhs, rhs_rows = [], {}
        for p, j in enumerate(cols):
            rr = LS * j if rev else LS * (j + 1) - 1
            ref = b2[rr:rr + 1]
            rows = slice(0, LS * j) if rev else slice(LS * (j + 1), C)
            scaled = (qf[rows] * jnp.exp2(b2[rows] - ref)).astype(BF16)
            pad = jnp.zeros((C - scaled.shape[0], HG_D), BF16)
            lhs.append(jnp.concatenate([scaled, pad] if rev else [pad, scaled], axis=0))
            kj = (kk[LS * j:LS * (j + 1)] * jnp.exp2(ref - b2[LS * j:LS * (j + 1)])).astype(BF16)
            rhs_rows[j] = jnp.concatenate([kj if i == p else zeros_l for i in range(nb - 1)], axis=1)
        kmat = jnp.concatenate([rhs_rows.get(j, zero_row) for j in range(nb)], axis=0)
        a_off.append(_dot_nt(jnp.concatenate(lhs, axis=1), kmat))

    lane = lax.broadcasted_iota(I32, (SUB, C), 1)
    a_diag = []
    for h in heads:
        qf, kk, b2, _ = front[h]
        tiles = [jnp.zeros((SUB, C), F32) for _ in range(C // SUB)]
        for s in range(C):
            blk = s // LS
            first, last = (LS * blk // SUB, s // SUB) if rev else (s // SUB, LS * (blk + 1) // SUB - 1)
            brow, krow = b2[s:s + 1], kk[s:s + 1]
            for t in range(first, last + 1):
                rs = slice(SUB * t, SUB * (t + 1))
                d = b2[rs] - brow
                if t == s // SUB:
                    d = jnp.minimum(d, 0.0)
                col = jnp.sum(qf[rs] * jnp.exp2(d) * krow, axis=1, keepdims=True)
                tiles[t] = jnp.where(lane == s, col, tiles[t])
        a_diag.append(jnp.where((r_i <= c_i) if rev else (r_i >= c_i), jnp.concatenate(tiles, axis=0), 0.0))

    return [(inter[h][0] + _dot((a_off[h] + a_diag[h]).astype(BF16), vs[h]), inter[h][1]) for h in heads]


def _hgrn_body(q_ref, z_ref, v_ref, lb_ref, o_ref, st_ref):
    d = pl.program_id(1)
    j = pl.program_id(2)
    nch = HG_BLOCK // HG_CHUNK
    hss = [slice(h * HG_D, (h + 1) * HG_D) for h in range(HG_HEADS)]

    @pl.when(j == 0)
    def _():
        st_ref[...] = jnp.zeros_like(st_ref)

    def run(rev):
        def step(i, carry):
            c = (nch - 1 - i) if rev else i
            rows = pl.ds(pl.multiple_of(c * HG_CHUNK, HG_CHUNK), HG_CHUNK)
            outs = _hgrn_chunks([q_ref[rows, hs] for hs in hss], [z_ref[rows, hs] for hs in hss],
                                [v_ref[rows, hs] for hs in hss],
                                [lb_ref[int(rev):int(rev) + 1, hs] for hs in hss],
                                [st_ref[h] for h in range(HG_HEADS)], rev)
            for h, (o, st) in enumerate(outs):
                o_ref[rows, hss[h]] = o
                st_ref[h] = st
            return carry
        lax.fori_loop(0, nch, step, 0, unroll=HG_UNROLL)

    @pl.when(d == 0)
    def _():
        run(False)

    @pl.when(d == 1)
    def _():
        run(True)


def _hgrn(hq, zf, hi, lb):
    bsz, t, _ = hq.shape
    nblk = t // HG_BLOCK

    def blk(d, j):
        return jnp.where(d == 0, j, nblk - 1 - j)

    tok = pl.BlockSpec((None, HG_BLOCK, HG_W), lambda b, d, j: (b, blk(d, j), 0))
    dirtok = pl.BlockSpec((None, None, HG_BLOCK, HG_W), lambda b, d, j: (d, b, blk(d, j), 0))
    return pl.pallas_call(
        _hgrn_body,
        grid=(bsz, 2, nblk),
        in_specs=[tok, dirtok, tok, _const_spec((2, HG_W))],
        out_specs=dirtok,
        out_shape=jax.ShapeDtypeStruct((2, bsz, t, HG_W), F32),
        scratch_shapes=[pltpu.VMEM((HG_HEADS, HG_D, HG_D), F32)],
        compiler_params=_cparams(("parallel", "arbitrary", "arbitrary")),
        name="hgrn",
    )(hq, zf, hi, lb)


def _attn_body(bound_s, q_ref, k_ref, v_ref, o_ref, qt_ref, s0_ref, m_ref, l_ref, acc_ref, *, unroll, bounded):
    tq, tks = ATT_TQ, s0_ref.shape[0]
    nsub = k_ref.shape[0] // tks
    for g in range(ATT_GROUP):
        qt_ref[:, g * tq:(g + 1) * tq] = q_ref[g]
    m_ref[...] = jnp.full_like(m_ref, -jnp.inf)
    l_ref[...] = jnp.zeros_like(l_ref)
    acc_ref[...] = jnp.zeros_like(acc_ref)
    qt = qt_ref[...]
    s0_ref[...] = _dot(k_ref[0:tks, :], qt)
    bound = bound_s[0]

    def group(i, carry):
        m, l, acc = m_ref[...], l_ref[...], acc_ref[...]
        s_next = s0_ref[...]
        for u in range(unroll):
            c = i * unroll + u
            s = s_next
            nxt = jnp.minimum(c + 1, nsub - 1)
            s_next = _dot(k_ref[pl.ds(pl.multiple_of(nxt * tks, tks), tks), :], qt)
            vt = v_ref[:, pl.ds(pl.multiple_of(c * tks, tks), tks)]
            if bounded:
                p = jnp.exp2(s - bound)
                l = l + jnp.sum(p, axis=0, keepdims=True)
                acc = acc + _dot(vt, p.astype(BF16))
            else:
                m_new = jnp.maximum(m, jnp.max(s, axis=0, keepdims=True))
                alpha = jnp.exp2(m - m_new)
                p = jnp.exp2(s - m_new)
                l = alpha * l + jnp.sum(p, axis=0, keepdims=True)
                acc = alpha * acc + _dot(vt, p.astype(BF16))
                m = m_new
        s0_ref[...] = s_next
        m_ref[...], l_ref[...], acc_ref[...] = m, l, acc
        return carry

    lax.fori_loop(0, nsub // unroll, group, 0)
    out = acc_ref[...] / l_ref[...]
    for g in range(ATT_GROUP):
        o_ref[:, g * ATT_HD:(g + 1) * ATT_HD] = out[:, g * tq:(g + 1) * tq].T.astype(BF16)


def _attn_call(bound, aq_t, ak, av_t, bounded):
    bsz, _, _, t = aq_t.shape
    tq = ATT_TQ
    tks, want = (ATT_TKS_BOUNDED, ATT_UNROLL_BOUNDED) if bounded else (ATT_TKS, ATT_UNROLL)
    nsub = t // tks
    unroll = max(u for u in (want, 8, 4, 2, 1) if u <= want and nsub % u == 0)
    gw = ATT_GROUP * ATT_HD
    return pl.pallas_call(
        functools.partial(_attn_body, unroll=unroll, bounded=bounded),
        grid_spec=pltpu.PrefetchScalarGridSpec(
            num_scalar_prefetch=1,
            grid=(bsz, ATT_KV, t // tq),
            in_specs=[pl.BlockSpec((None, ATT_GROUP, ATT_HD, tq), lambda b, h, qi, *_: (b, h, 0, qi)),
                      pl.BlockSpec((None, t, ATT_HD), lambda b, h, qi, *_: (b, 0, h)),
                      pl.BlockSpec((None, None, ATT_HD, t), lambda b, h, qi, *_: (b, h, 0, 0))],
            out_specs=pl.BlockSpec((None, tq, gw), lambda b, h, qi, *_: (b, qi, h)),
            scratch_shapes=[pltpu.VMEM((ATT_HD, ATT_GROUP * tq), BF16),
                            pltpu.VMEM((tks, ATT_GROUP * tq), F32),
                            pltpu.VMEM((1, ATT_GROUP * tq), F32),
                            pltpu.VMEM((1, ATT_GROUP * tq), F32),
                            pltpu.VMEM((ATT_HD, ATT_GROUP * tq), F32)]),
        out_shape=jax.ShapeDtypeStruct((bsz, t, ATT_W), BF16),
        compiler_params=_cparams(("parallel", "parallel", "arbitrary")),
        name="attn_bounded" if bounded else "attn",
    )(bound, aq_t, ak, av_t)


def _attn(bound, aq_t, ak, av_t):
    return lax.cond(bound[0] <= ATT_MAX_BOUNDED_SCORE,
                    functools.partial(_attn_call, bounded=True),
                    functools.partial(_attn_call, bounded=False),
                    bound, aq_t, ak, av_t)


def _score_bound(q_gain, k_gain):
    qscale = (ATT_HD ** -0.5) * LOG2E
    b = 1.01 * ATT_HD * qscale * jnp.max(jnp.abs(q_gain)) * jnp.max(jnp.abs(k_gain))
    return b.reshape(1).astype(F32)


def _memkv_body(m_ref, nw_ref, w_ref, o_ref):
    o_ref[...] = _dot(_rms(m_ref[...], nw_ref[...]).astype(BF16), w_ref[...]).astype(BF16)


def _memkv(mem, nw, w_ckv):
    bsz, m, d = mem.shape
    return pl.pallas_call(
        _memkv_body,
        grid=(bsz,),
        in_specs=[pl.BlockSpec((None, m, d), lambda b: (b, 0, 0)), _const_spec((1, d)),
                  _const_spec((d, 2 * X_W))],
        out_specs=pl.BlockSpec((None, m, 2 * X_W), lambda b: (b, 0, 0)),
        out_shape=jax.ShapeDtypeStruct((bsz, m, 2 * X_W), BF16),
        compiler_params=_cparams(("parallel",)),
        name="memkv",
    )(mem, nw, w_ckv)


def _post_body(x_ref, o2_ref, hg_ref, oa_ref, gh_ref, ga_ref, kv_ref,
               hgn_ref, wbh_ref, wba_ref, wo_ref, nx_ref, wcq_ref, wco_ref, nf_ref, wr_ref,
               h2_ref, u3_ref, aff_ref):
    o = o2_ref[0] + o2_ref[1]
    parts = []
    for h in range(HG_HEADS):
        oh = o[:, h * HG_D:(h + 1) * HG_D]
        parts.append(oh * lax.rsqrt(jnp.mean(oh * oh, axis=-1, keepdims=True) + EPS))
    hgv = hg_ref[...]
    oh = (jnp.concatenate(parts, axis=1) * hgn_ref[...] * (hgv * _sigmoid(hgv))).astype(BF16)
    merged = (_sigmoid(gh_ref[...]) * _dot(oh, wbh_ref[...])
              + _sigmoid(ga_ref[...]) * _dot(oa_ref[...], wba_ref[...]))
    h1 = x_ref[...] + _dot(merged.astype(BF16), wo_ref[...])

    u2 = _rms(h1, nx_ref[...]).astype(BF16)
    qx = (_dot(u2, wcq_ref[...]) * (X_HD ** -0.5)).astype(BF16)
    outs = []
    for h in range(X_HEADS):
        hs = slice(h * X_HD, (h + 1) * X_HD)
        s = _dot_nt(qx[:, hs], kv_ref[:, hs])
        p = jnp.exp(s - jnp.max(s, axis=1, keepdims=True))
        ov = _dot(p.astype(BF16), kv_ref[:, X_W + h * X_HD:X_W + (h + 1) * X_HD])
        outs.append(ov / jnp.sum(p, axis=1, keepdims=True))
    h2 = h1 + _dot(jnp.concatenate(outs, axis=1).astype(BF16), wco_ref[...])
    h2_ref[...] = h2

    u3 = _rms(h2, nf_ref[...]).astype(BF16)
    u3_ref[...] = u3
    logits = _dot_nt(wr_ref[...], u3)
    e = jnp.exp(logits - jnp.max(logits, axis=0, keepdims=True))
    aff_ref[...] = e / jnp.sum(e, axis=0, keepdims=True)


def _post(x, o2, hg, oa, gh, ga, kv, hgn, wbh, wba, wo, nx, wcq, wco, nf, wr_t):
    bsz, t, d = x.shape
    tm = POST_TILE
    nt = t // tm
    m = kv.shape[1]
    tok = lambda w: pl.BlockSpec((None, tm, w), lambda b, i: (b, i, 0))
    return pl.pallas_call(
        _post_body,
        grid=(bsz, nt),
        in_specs=[tok(d), pl.BlockSpec((2, None, tm, HG_W), lambda b, i: (0, b, i, 0)), tok(HG_W),
                  tok(ATT_W), tok(d), tok(d), pl.BlockSpec((None, m, 2 * X_W), lambda b, i: (b, 0, 0)),
                  _const_spec((1, HG_W)), _const_spec((HG_W, d)), _const_spec((ATT_W, d)),
                  _const_spec((d, d)), _const_spec((1, d)), _const_spec((d, X_W)), _const_spec((X_W, d)),
                  _const_spec((1, d)), _const_spec((N_EXPERTS, d))],
        out_specs=[tok(d), tok(d), pl.BlockSpec((N_EXPERTS, tm), lambda b, i: (0, b * nt + i))],
        out_shape=[jax.ShapeDtypeStruct((bsz, t, d), F32), jax.ShapeDtypeStruct((bsz, t, d), BF16),
                   jax.ShapeDtypeStruct((N_EXPERTS, bsz * t), F32)],
        compiler_params=_cparams(("parallel", "parallel")),
        name="post",
    )(x, o2, hg, oa, gh, ga, kv, hgn, wbh, wba, wo, nx, wcq, wco, nf, wr_t)


def _route_body(aff_ref, pos_ref, off_ref, cnt_ref, *, cap):
    n = aff_ref.shape[1]
    nblk = n // ROUTE_BLOCK
    bits = pltpu.bitcast(aff_ref[...], I32)

    def count_ge(th):
        return jnp.sum(jnp.where(bits >= th, 1.0, 0.0), axis=1, keepdims=True)

    def bisect(_, lohi):
        lo, hi = lohi
        mid = lo + lax.shift_right_logical(hi - lo, 1)
        ok = count_ge(mid) >= cap
        return jnp.where(ok, mid, lo), jnp.where(ok, hi, mid)

    lo0 = jnp.zeros((N_EXPERTS, 1), I32)
    hi0 = jnp.full((N_EXPERTS, 1), 0x7F800000, I32)
    thr, _ = lax.fori_loop(0, 31, bisect, (lo0, hi0))
    n_gt = jnp.sum(jnp.where(bits > thr, 1.0, 0.0), axis=1, keepdims=True)
    need = cap - n_gt

    r_i = lax.broadcasted_iota(I32, (ROUTE_BLOCK, ROUTE_BLOCK), 0)
    c_i = lax.broadcasted_iota(I32, (ROUTE_BLOCK, ROUTE_BLOCK), 1)
    tri = jnp.where(r_i < c_i, 1.0, 0.0).astype(BF16)
    lane = lax.broadcasted_iota(I32, (N_EXPERTS, MAX_ROUTE_BLOCKS), 1)

    off_ref[...] = jnp.zeros_like(off_ref)
    cnt_ref[...] = jnp.zeros_like(cnt_ref)

    def scan(j, carry):
        run_sel, run_eq = carry
        cols = pl.ds(pl.multiple_of(j * ROUTE_BLOCK, ROUTE_BLOCK), ROUTE_BLOCK)
        b = pltpu.bitcast(aff_ref[:, cols], I32)
        eq = jnp.where(b == thr, 1.0, 0.0)
        eq_rank = run_eq + _dot(eq.astype(BF16), tri)
        sel = jnp.where(b > thr, 1.0, jnp.where(eq_rank < need, eq, 0.0))
        pos = run_sel + _dot(sel.astype(BF16), tri)
        pos_ref[:, cols] = jnp.where(sel > 0.0, pos, -1.0).astype(I32)
        c = jnp.sum(sel, axis=1, keepdims=True)
        off_ref[...] = jnp.where(lane == j, run_sel.astype(I32), off_ref[...])
        cnt_ref[...] = jnp.where(lane == j, c.astype(I32), cnt_ref[...])
        return run_sel + c, run_eq + jnp.sum(eq, axis=1, keepdims=True)

    z1 = n_gt * 0.0
    lax.fori_loop(0, nblk, scan, (z1, z1))


def _route(aff_t, cap):
    n = aff_t.shape[1]
    assert n % ROUTE_BLOCK == 0 and n // ROUTE_BLOCK <= MAX_ROUTE_BLOCKS
    tbl = jax.ShapeDtypeStruct((N_EXPERTS, MAX_ROUTE_BLOCKS), I32)
    return pl.pallas_call(
        functools.partial(_route_body, cap=cap),
        out_shape=[jax.ShapeDtypeStruct((N_EXPERTS, n), I32), tbl, tbl],
        compiler_params=pltpu.CompilerParams(vmem_limit_bytes=V7X_VMEM_LIMIT_BYTES),
        name="route",
    )(aff_t)


def _dispatch_body(off_s, cnt_s, pos_ref, aff_ref, u_ref, xe_ref, stage_ref, xstage_ref, tail_ref, sem_ref,
                   xsem_ref, *, cap_pad):
    j = pl.program_id(0)
    last = pl.num_programs(0) - 1
    slot = j % 2

    @pl.when(j == 0)
    def _():
        tail_ref[...] = jnp.zeros_like(tail_ref)

    def geom(e, jj):
        off = off_s[e * MAX_ROUTE_BLOCKS + jj]
        c = cnt_s[e * MAX_ROUTE_BLOCKS + jj]
        a = off & (ROW_ALIGN - 1)
        return off - a, a + c, jnp.where(c > 0, (a + c + ROW_CHUNK - 1) // ROW_CHUNK, 0)

    def dst(e, row):
        return xe_ref.at[pl.ds(pl.multiple_of(e * cap_pad + row, ROW_ALIGN), ROW_CHUNK), :]

    def first_chunks(jj, sl):
        return [pltpu.make_async_copy(stage_ref.at[sl, pl.ds(e * ROW_CHUNK, ROW_CHUNK), :],
                                      dst(e, geom(e, jj)[0]), sem_ref.at[sl]) for e in range(N_EXPERTS)]

    terms = [t.astype(F32) for t in _split3(aff_ref[...].T)]
    fill = jnp.zeros((ROUTE_BLOCK, GATE_LANES - GATE_TERMS * N_EXPERTS), F32)
    u = jnp.concatenate([u_ref[...], jnp.concatenate(terms + [fill], axis=1).astype(BF16)], axis=1)
    r_i = lax.broadcasted_iota(I32, (ROW_CHUNK, ROUTE_BLOCK), 0)

    def onehot(e, row0):
        return jnp.where(r_i == pos_ref[pl.ds(e, 1), :] - row0, 1.0, 0.0).astype(BF16)

    rows = _dot(jnp.concatenate([onehot(e, geom(e, j)[0]) for e in range(N_EXPERTS)], axis=0), u)
    for e in range(N_EXPERTS):
        r0 = e * ROW_CHUNK
        stage_ref[slot, r0:r0 + ROW_ALIGN, :] = (rows[r0:r0 + ROW_ALIGN] + tail_ref[e].astype(F32)).astype(BF16)
        stage_ref[slot, r0 + ROW_ALIGN:r0 + ROW_CHUNK, :] = rows[r0 + ROW_ALIGN:r0 + ROW_CHUNK].astype(BF16)

    @pl.when(j > 0)
    def _():
        for cp in first_chunks(j - 1, 1 - slot):
            cp.wait()

    for cp in first_chunks(j, slot):
        cp.start()

    def expert(e, carry):
        base, end, nch = geom(e, j)

        def extra(ch, carry):
            row0 = base + ch * ROW_CHUNK
            xstage_ref[...] = _dot(onehot(e, row0), u).astype(BF16)
            cp = pltpu.make_async_copy(xstage_ref, dst(e, row0), xsem_ref)
            cp.start()
            cp.wait()
            return carry

        lax.fori_loop(1, jnp.maximum(nch, 1), extra, carry)

        @pl.when(nch > 0)
        def _():
            part = end & (ROW_ALIGN - 1)
            r0 = pl.multiple_of((end - part) & (ROW_CHUNK - 1), ROW_ALIGN)
            keep = jnp.where(nch > 1, xstage_ref[pl.ds(r0, ROW_ALIGN), :],
                             stage_ref[slot, pl.ds(pl.multiple_of(e * ROW_CHUNK + r0, ROW_ALIGN), ROW_ALIGN), :])
            tail_ref[e] = jnp.where(part > 0, keep, jnp.zeros_like(keep))

        return carry

    lax.fori_loop(0, N_EXPERTS, expert, 0)

    @pl.when(j == last)
    def _():
        for cp in first_chunks(j, slot):
            cp.wait()
        xstage_ref[...] = jnp.zeros_like(xstage_ref)
        pads = [pltpu.make_async_copy(xstage_ref, dst(e, cap_pad - ROW_CHUNK), xsem_ref) for e in range(N_EXPERTS)]
        for cp in pads:
            cp.start()
        for cp in pads:
            cp.wait()


def _dispatch(off, cnt, pos_t, aff_t, u, cap):
    n, d = u.shape
    d_aug = d + GATE_LANES
    cap_pad = cap + ROW_CHUNK
    nblk = n // ROUTE_BLOCK
    eblk = pl.BlockSpec((N_EXPERTS, ROUTE_BLOCK), lambda j, *_: (0, j))
    return pl.pallas_call(
        functools.partial(_dispatch_body, cap_pad=cap_pad),
        grid_spec=pltpu.PrefetchScalarGridSpec(
            num_scalar_prefetch=2,
            grid=(nblk,),
            in_specs=[eblk, eblk, pl.BlockSpec((ROUTE_BLOCK, d), lambda j, *_: (j, 0))],
            out_specs=pl.BlockSpec(memory_space=pl.ANY),
            scratch_shapes=[pltpu.VMEM((2, N_EXPERTS * ROW_CHUNK, d_aug), BF16),
                            pltpu.VMEM((ROW_CHUNK, d_aug), BF16),
                            pltpu.VMEM((N_EXPERTS, ROW_ALIGN, d_aug), BF16),
                            pltpu.SemaphoreType.DMA((2,)),
                            pltpu.SemaphoreType.DMA(())]),
        out_shape=jax.ShapeDtypeStruct((N_EXPERTS * cap_pad, d_aug), BF16),
        compiler_params=_cparams(("arbitrary",)),
        name="dispatch",
    )(off, cnt, pos_t, aff_t, u)


_FFN_COLS = 512


def _ffn_body(x_ref, wg_ref, wu_ref, wd_ref, o_ref):
    x = x_ref[:, :D_MODEL]
    acc = None
    for c in range(D_EXPERT // _FFN_COLS):
        cs = slice(c * _FFN_COLS, (c + 1) * _FFN_COLS)
        g = _dot(x, wg_ref[:, cs])
        hid = (g * _sigmoid(g) * _dot(x, wu_ref[:, cs])).astype(BF16)
        part = _dot(hid, wd_ref[cs, :])
        acc = part if acc is None else acc + part
    lane = lax.broadcasted_iota(I32, (1, GATE_LANES), 1)
    mine = (lane < GATE_TERMS * N_EXPERTS) & (lane % N_EXPERTS == pl.program_id(0))
    gate = jnp.sum(jnp.where(mine, x_ref[:, D_MODEL:].astype(F32), 0.0), axis=1, keepdims=True)
    o_ref[...] = (acc * gate).astype(BF16)


def _ffn(xe, wg, wu, wd, cap):
    e, _, d_aug = xe.shape
    d = d_aug - GATE_LANES
    tm = 512 if cap % 512 == 0 else ROW_CHUNK
    wspec = lambda a, b: pl.BlockSpec((None, a, b), lambda ei, i: (ei, 0, 0))
    return pl.pallas_call(
        _ffn_body,
        grid=(e, cap // tm),
        in_specs=[pl.BlockSpec((None, tm, d_aug), lambda ei, i: (ei, i, 0)),
                  wspec(d, D_EXPERT), wspec(d, D_EXPERT), wspec(D_EXPERT, d)],
        out_specs=pl.BlockSpec((None, tm, d), lambda ei, i: (ei, i, 0)),
        out_shape=jax.ShapeDtypeStruct((e, cap, d), BF16),
        compiler_params=_cparams(("parallel", "parallel")),
        name="ffn",
    )(xe, wg, wu, wd)


def _combine_body(off_s, cnt_s, pos_ref, h_ref, nw_ref, ye_ref, y_ref, rows_ref, lhs_ref, xbuf_ref,
                  sem_ref, xsem_ref, acc_ref, *, cap):
    j = pl.program_id(0)
    nblk = pl.num_programs(0)
    per_e = ROW_CHUNK

    def geom(e, jj):
        off = off_s[e * MAX_ROUTE_BLOCKS + jj]
        c = cnt_s[e * MAX_ROUTE_BLOCKS + jj]
        a = off & (ROW_ALIGN - 1)
        return off - a, jnp.where(c > 0, (a + c + ROW_CHUNK - 1) // ROW_CHUNK, 0)

    def window(base, ch):
        lo = base + ch * ROW_CHUNK
        return lo, pl.multiple_of(jnp.minimum(lo, cap - ROW_CHUNK), ROW_ALIGN)

    def src(e, start):
        return ye_ref.at[pl.ds(pl.multiple_of(e * cap + start, ROW_ALIGN), ROW_CHUNK), :]

    def first_chunks(jj, slot):
        cps = []
        for e in range(N_EXPERTS):
            base, _ = geom(e, jj)
            _, start = window(base, 0)
            cps.append(pltpu.make_async_copy(
                src(e, start), rows_ref.at[slot, pl.ds(e * per_e, ROW_CHUNK), :], sem_ref.at[slot]))
        return cps

    @pl.when(j == 0)
    def _():
        for cp in first_chunks(0, 0):
            cp.start()

    @pl.when(j + 1 < nblk)
    def _():
        for cp in first_chunks(j + 1, (j + 1) % 2):
            cp.start()

    pos_c = pos_ref[...].T
    lane = lax.broadcasted_iota(I32, (ROUTE_BLOCK, ROW_CHUNK), 1)
    for e in range(N_EXPERTS):
        base, _ = geom(e, j)
        lo, start = window(base, 0)
        r = pos_c[:, e:e + 1] - start
        hit = lane == jnp.where(r >= lo - start, r, -1)
        lhs_ref[:, e * per_e:(e + 1) * per_e] = jnp.where(hit, 1.0, 0.0).astype(BF16)

    slot = j % 2
    for cp in first_chunks(j, slot):
        cp.wait()
    acc_ref[...] = _dot(lhs_ref[...], rows_ref[slot])

    r_i = lax.broadcasted_iota(I32, (ROW_CHUNK, ROUTE_BLOCK), 0)

    def expert(e, carry):
        base, nch = geom(e, j)

        def extra(ch, carry):
            lo, start = window(base, ch)
            cp = pltpu.make_async_copy(src(e, start), xbuf_ref, xsem_ref)
            cp.start()
            cp.wait()
            pos = pos_ref[pl.ds(e, 1), :]
            hit = (r_i == pos - start) & (pos >= lo)
            acc_ref[...] += _dot_tn(jnp.where(hit, 1.0, 0.0).astype(BF16), xbuf_ref[...])
            return carry

        return lax.fori_loop(1, jnp.maximum(nch, 1), extra, carry)

    lax.fori_loop(0, N_EXPERTS, expert, 0)
    y_ref[...] = _rms(h_ref[...] + acc_ref[...], nw_ref[...])


def _combine(off, cnt, pos_t, h2, nw, ye, cap):
    n, d = h2.shape
    nblk = n // ROUTE_BLOCK
    eblk = pl.BlockSpec((N_EXPERTS, ROUTE_BLOCK), lambda j, *_: (0, j))
    tok = pl.BlockSpec((ROUTE_BLOCK, d), lambda j, *_: (j, 0))
    kdim = N_EXPERTS * ROW_CHUNK
    return pl.pallas_call(
        functools.partial(_combine_body, cap=cap),
        grid_spec=pltpu.PrefetchScalarGridSpec(
            num_scalar_prefetch=2,
            grid=(nblk,),
            in_specs=[eblk, tok, pl.BlockSpec((1, d), lambda j, *_: (0, 0)),
                      pl.BlockSpec(memory_space=pl.ANY)],
            out_specs=tok,
            scratch_shapes=[pltpu.VMEM((2, kdim, d), BF16),
                            pltpu.VMEM((ROUTE_BLOCK, kdim), BF16),
                            pltpu.VMEM((ROW_CHUNK, d), BF16),
                            pltpu.SemaphoreType.DMA((2,)),
                            pltpu.SemaphoreType.DMA(()),
                            pltpu.VMEM((ROUTE_BLOCK, d), F32)]),
        out_shape=jax.ShapeDtypeStruct((n, d), F32),
        compiler_params=_cparams(("arbitrary",)),
        name="combine",
    )(off, cnt, pos_t, h2, nw, ye)


def _trunk(x, mem, lb, p):
    bsz, t, d = x.shape
    n = bsz * t
    cap = max(1, EC_FACTOR * n // N_EXPERTS)
    assert cap % ROW_CHUNK == 0 and t % HG_BLOCK == 0 and t % ATT_TKS == 0
    cos, s1, s2 = _rope_tables(t)
    hq, zf, hi, hg, aq, ak, av, gh, ga = _proj(x, p["norm_mix"], p["w_in"], cos, s1, s2, p["q_norm"], p["k_norm"])
    o2 = _hgrn(hq, zf, hi, lb)
    oa = _attn(p["score_bound"], aq, ak, av)
    kv = _memkv(mem, p["norm_mem"], p["w_ckv"])
    h2, u3, aff_t = _post(x, o2, hg, oa, gh, ga, kv, p["hgrn_norm"], p["w_br_hgrn"], p["w_br_attn"], p["w_out"],
                          p["norm_x"], p["w_cq"], p["w_co"], p["norm_ffn"], p["w_router_t"])
    pos_t, off, cnt = _route(aff_t, cap)
    off, cnt = off.reshape(-1), cnt.reshape(-1)
    xe = _dispatch(off, cnt, pos_t, aff_t, u3.reshape(n, d), cap)
    ye = _ffn(xe.reshape(N_EXPERTS, cap + ROW_CHUNK, d + GATE_LANES), p["w_gate"], p["w_up"], p["w_down"], cap)
    y = _combine(off, cnt, pos_t, h2.reshape(n, d), p["norm_final"], ye.reshape(N_EXPERTS * cap, d), cap)
    return y.reshape(bsz, t, d)


def kernel(x_prompt, x_sample, mem_prompt, mem_sample, norm_mix, w_in, lb_logits, hgrn_norm, q_norm, k_norm,
           w_br_hgrn, w_br_attn, w_out, norm_x, norm_mem, w_cq, w_ckv, w_co, norm_ffn, w_router, w_gate, w_up,
           w_down, norm_final):
    row = lambda v: v.reshape(1, -1).astype(F32)
    p = {
        "norm_mix": row(norm_mix[0]), "w_in": w_in[0].astype(BF16),
        "hgrn_norm": row(hgrn_norm[0]), "q_norm": row(q_norm[0]), "k_norm": row(k_norm[0]),
        "w_br_hgrn": w_br_hgrn[0].astype(BF16), "w_br_attn": w_br_attn[0].astype(BF16),
        "w_out": w_out[0].astype(BF16), "norm_x": row(norm_x[0]), "norm_mem": row(norm_mem[0]),
        "w_cq": w_cq[0].astype(BF16), "w_ckv": w_ckv[0].astype(BF16), "w_co": w_co[0].astype(BF16),
        "norm_ffn": row(norm_ffn[0]), "w_router_t": w_router[0].T.astype(BF16),
        "w_gate": w_gate[0].astype(BF16), "w_up": w_up[0].astype(BF16), "w_down": w_down[0].astype(BF16),
        "norm_final": row(norm_final), "score_bound": _score_bound(q_norm[0], k_norm[0]),
    }
    lb = jnp.cumsum(jax.nn.softmax(lb_logits.astype(F32), axis=1), axis=1)[:, 0]
    return (_trunk(x_prompt, mem_prompt, lb, p), _trunk(x_sample, mem_sample, lb, p))
```
